```python
import jax, jax.numpy as jnp
from jax import lax
import numpy as np

D_MODEL = 2048
BATCH = 4
SEQ = 4096
DEPTH = 1
DEC_BATCH = 4
DEC_SEQ = 2048
PAST_LEN = 128

MIX_WIDTH = D_MODEL
HEAD_DIM = 128
N_HEADS_TOTAL = MIX_WIDTH // HEAD_DIM
N_HEADS_HGRN = N_HEADS_TOTAL // 2
N_HEADS_ATTN = N_HEADS_TOTAL - N_HEADS_HGRN
HGRN_DK = 128
HGRN_DV = HEAD_DIM
HGRN_FWIDTH = N_HEADS_HGRN * HGRN_DK
HGRN_WIDTH = N_HEADS_HGRN * HGRN_DV
ATTN_WIDTH = N_HEADS_ATTN * HEAD_DIM
IN_SPLITS = (HGRN_FWIDTH, HGRN_FWIDTH, HGRN_FWIDTH, HGRN_WIDTH, HGRN_WIDTH,
             ATTN_WIDTH, ATTN_WIDTH, ATTN_WIDTH)
IN_WIDTH = sum(IN_SPLITS)
CHUNK = 64
DILATED_PATTERNS = ((128, 1), (512, 4), (2048, 16))
ATTN_BLOCK = 64
D_FF = 5632
ALPHA = (2 * DEPTH) ** 0.25
BETA_INIT = (8 * DEPTH) ** -0.25
LN_EPS = 1e-5
RMS_EPS = 1e-6
NEG_INF = -1e30

kernel_name = "hybrid_hgrn2_dilated_alibi_encoder"


def layer_norm(x, g, b):
    xf = x.astype(jnp.float32)
    mu = jnp.mean(xf, axis=-1, keepdims=True)
    var = jnp.mean(jnp.square(xf - mu), axis=-1, keepdims=True)
    return ((xf - mu) * lax.rsqrt(var + LN_EPS) * g.astype(jnp.float32) + b.astype(jnp.float32)).astype(x.dtype)


def swiglu(x, w_gate, w_up, w_down):
    return (jax.nn.silu(x @ w_gate) * (x @ w_up)) @ w_down


def hgrn2_direction(q, k, v, logf):
    B, S, H, DK = q.shape
    DV = v.shape[-1]
    n = S // CHUNK

    def chunks(a):
        return a.reshape(B, n, CHUNK, H, a.shape[-1]).transpose(1, 0, 3, 2, 4)

    tri = jnp.tril(jnp.ones((CHUNK, CHUNK), dtype=bool))

    def step(state, inp):
        qc, kc, vc, lf = inp
        G = jnp.cumsum(lf, axis=2)
        inter = jnp.einsum('bhtk,bhkv->bhtv', qc * jnp.exp(G), state)
        diff = G[:, :, :, None, :] - G[:, :, None, :, :]
        decay = jnp.where(tri[:, :, None], jnp.exp(jnp.minimum(diff, 0.0)), 0.0)
        A = jnp.einsum('bhtk,bhsk,bhtsk->bhts', qc, kc, decay)
        intra = jnp.einsum('bhts,bhsv->bhtv', A, vc)
        g_last = G[:, :, -1, :]
        new_state = (jnp.exp(g_last)[..., None] * state
                     + jnp.einsum('bhsk,bhsv->bhkv', kc * jnp.exp(g_last[:, :, None, :] - G), vc))
        return new_state, inter + intra

    state0 = jnp.zeros((B, H, DK, DV), jnp.float32)
    _, o = lax.scan(step, state0, (chunks(q), chunks(k), chunks(v), chunks(logf)))
    return o.transpose(1, 0, 3, 2, 4).reshape(B, S, H, DV)


def dilated_window_attention(q, k, v, slopes, window, dilation):
    B, S, H, hd = q.shape
    half = (window // 2) // dilation
    blk = ATTN_BLOCK
    L = S // dilation
    BD = B * dilation

    def split(a):
        return a.reshape(B, L, dilation, H, hd).transpose(0, 2, 1, 3, 4).reshape(BD, L, H, hd)

    qs, ks, vs = split(q), split(k), split(v)
    nb = -(-L // blk)
    Lp = nb * blk
    pad = Lp - L
    qb = jnp.pad(qs, ((0, 0), (0, pad), (0, 0), (0, 0))).reshape(BD, nb, blk, H, hd)

    def windows(a):
        ap = jnp.pad(a, ((0, 0), (blk, blk + pad), (0, 0), (0, 0))).reshape(BD, nb + 2, blk, H, hd)
        return jnp.concatenate([ap[:, :-2], ap[:, 1:-1], ap[:, 2:]], axis=2)

    kw, vw = windows(ks), windows(vs)
    rel = jnp.arange(3 * blk)[None, :] - blk - jnp.arange(blk)[:, None]
    jpos = jnp.arange(nb)[:, None, None] * blk + rel[None]
    valid = (jpos >= 0) & (jpos < L) & (jnp.abs(rel)[None] <= half)
    dist = (dilation * jnp.abs(rel)).astype(jnp.float32)
    bias = -slopes[:, None, None] * dist[None]

    scores = jnp.einsum('znqhd,znkhd->znhqk', qb, kw,
                        preferred_element_type=jnp.float32) * (hd ** -0.5) + bias[None, None]
    scores = jnp.where(valid[None, :, None], scores, NEG_INF)
    m = jnp.max(scores, axis=-1, keepdims=True)
    p = jnp.exp(scores - m)
    den = jnp.sum(p, axis=-1)
    out = jnp.einsum('znhqk,znkhd->znqhd', p.astype(v.dtype), vw,
                     preferred_element_type=jnp.float32) / den.transpose(0, 1, 3, 2)[..., None]
    lse = (m[..., 0] + jnp.log(den)).transpose(0, 1, 3, 2)

    out = out.reshape(BD, Lp, H, hd)[:, :L]
    lse = lse.reshape(BD, Lp, H)[:, :L]
    out = out.reshape(B, dilation, L, H, hd).transpose(0, 2, 1, 3, 4).reshape(B, S, H, hd)
    lse = lse.reshape(B, dilation, L, H).transpose(0, 2, 1, 3).reshape(B, S, H)
    return out, lse


def hybrid_mixer(h, w_in, lb_fwd, lb_bwd, norm_g, w_out):
    B, S, _ = h.shape
    f32 = jnp.float32
    proj = h @ w_in
    cuts = list(np.cumsum(IN_SPLITS)[:-1])
    q_h, zf_f, zf_b, i_h, g_h, q_a, k_a, v_a = jnp.split(proj, cuts, axis=-1)

    def heads(a, d):
        return a.reshape(B, S, N_HEADS_HGRN, d).astype(f32)

    def gates(z, lb):
        f = lb + (1.0 - lb) * jax.nn.sigmoid(z.astype(f32))
        return heads(jnp.log(f), HGRN_DK), heads(1.0 - f, HGRN_DK)

    qh = heads(q_h, HGRN_DK)
    vh = heads(i_h, HGRN_DV)
    logf_f, k_f = gates(zf_f, lb_fwd)
    logf_b, k_b = gates(zf_b, lb_bwd)
    o_f = hgrn2_direction(qh, k_f, vh, logf_f)
    flip = lambda a: jnp.flip(a, axis=1)
    o_b = flip(hgrn2_direction(flip(qh), flip(k_b), flip(vh), flip(logf_b)))
    o = o_f + o_b
    o = o * lax.rsqrt(jnp.mean(jnp.square(o), axis=-1, keepdims=True) + RMS_EPS)
    hgrn_out = (o.reshape(B, S, HGRN_WIDTH) * norm_g.astype(f32)
                * jax.nn.silu(g_h.astype(f32))).astype(h.dtype)

    qa = q_a.reshape(B, S, N_HEADS_ATTN, HEAD_DIM)
    ka = k_a.reshape(B, S, N_HEADS_ATTN, HEAD_DIM)
    va = v_a.reshape(B, S, N_HEADS_ATTN, HEAD_DIM)
    slopes = 2.0 ** (-8.0 * (jnp.arange(N_HEADS_ATTN, dtype=f32) + 1.0) / N_HEADS_ATTN)
    outs, lses = [], []
    for window, dilation in DILATED_PATTERNS:
        o_p, l_p = dilated_window_attention(qa, ka, va, slopes, window, dilation)
        outs.append(o_p)
        lses.append(l_p)
    wts = jax.nn.softmax(jnp.stack(lses, axis=0), axis=0)
    attn = jnp.sum(wts[..., None] * jnp.stack(outs, axis=0), axis=0)
    attn_out = attn.reshape(B, S, ATTN_WIDTH).astype(h.dtype)

    return jnp.concatenate([hgrn_out, attn_out], axis=-1) @ w_out


def encoder_trunk(x, ln1_g, ln1_b, ffn1_w_gate, ffn1_w_up, ffn1_w_down,
                  ln2_g, ln2_b, w_in, hgrn_lb_fwd, hgrn_lb_bwd, hgrn_norm_g, w_out,
                  ln3_g, ln3_b, ffn2_w_gate, ffn2_w_up, ffn2_w_down):
    lbs_f = jnp.cumsum(jax.nn.softmax(hgrn_lb_fwd.astype(jnp.float32), axis=0), axis=0)
    lbs_b = jnp.cumsum(jax.nn.softmax(hgrn_lb_bwd.astype(jnp.float32), axis=0), axis=0)
    for l in range(DEPTH):
        x = layer_norm(ALPHA * x + 0.5 * swiglu(x, ffn1_w_gate[l], ffn1_w_up[l], ffn1_w_down[l]),
                       ln1_g[l], ln1_b[l])
        x = layer_norm(ALPHA * x + hybrid_mixer(x, w_in[l], lbs_f[l], lbs_b[l], hgrn_norm_g[l], w_out[l]),
                       ln2_g[l], ln2_b[l])
        x = layer_norm(ALPHA * x + 0.5 * swiglu(x, ffn2_w_gate[l], ffn2_w_up[l], ffn2_w_down[l]),
                       ln3_g[l], ln3_b[l])
    return x


def setup_inputs(seed: int = 0) -> dict:
    key = jax.random.key(seed)
    ks = jax.random.split(key, 24)
    f32 = jnp.float32
    nrm = lambda k, shape, s: jax.random.normal(k, shape, f32) * s
    gain = lambda k: 1.0 + nrm(k, (DEPTH, D_MODEL), 0.02)
    bias = lambda k: nrm(k, (DEPTH, D_MODEL), 0.02)
    return {
        "x_prompt": nrm(ks[0], (BATCH, SEQ, D_MODEL), 1.0),
        "x_sample": nrm(ks[1], (DEC_BATCH, DEC_SEQ, D_MODEL), 1.0),
        "ln1_g": gain(ks[2]),
        "ln1_b": bias(ks[3]),
        "ffn1_w_gate": nrm(ks[4], (DEPTH, D_MODEL, D_FF), D_MODEL ** -0.5),
        "ffn1_w_up": nrm(ks[5], (DEPTH, D_MODEL, D_FF), D_MODEL ** -0.5),
        "ffn1_w_down": nrm(ks[6], (DEPTH, D_FF, D_MODEL), BETA_INIT * D_FF ** -0.5),
        "ln2_g": gain(ks[7]),
        "ln2_b": bias(ks[8]),
        "w_in": nrm(ks[9], (DEPTH, D_MODEL, IN_WIDTH), D_MODEL ** -0.5),
        "hgrn_lb_fwd": nrm(ks[10], (DEPTH + 1, HGRN_FWIDTH), 0.5),
        "hgrn_lb_bwd": nrm(ks[11], (DEPTH + 1, HGRN_FWIDTH), 0.5),
        "hgrn_norm_g": 1.0 + nrm(ks[12], (DEPTH, HGRN_WIDTH), 0.02),
        "w_out": nrm(ks[13], (DEPTH, MIX_WIDTH, D_MODEL), BETA_INIT * MIX_WIDTH ** -0.5),
        "ln3_g": gain(ks[14]),
        "ln3_b": bias(ks[15]),
        "ffn2_w_gate": nrm(ks[16], (DEPTH, D_MODEL, D_FF), D_MODEL ** -0.5),
        "ffn2_w_up": nrm(ks[17], (DEPTH, D_MODEL, D_FF), D_MODEL ** -0.5),
        "ffn2_w_down": nrm(ks[18], (DEPTH, D_FF, D_MODEL), BETA_INIT * D_FF ** -0.5),
    }


def reference(x_prompt, x_sample, ln1_g, ln1_b, ffn1_w_gate, ffn1_w_up, ffn1_w_down,
              ln2_g, ln2_b, w_in, hgrn_lb_fwd, hgrn_lb_bwd, hgrn_norm_g, w_out,
              ln3_g, ln3_b, ffn2_w_gate, ffn2_w_up, ffn2_w_down):
    y_prompt = encoder_trunk(x_prompt, ln1_g, ln1_b, ffn1_w_gate, ffn1_w_up, ffn1_w_down,
                             ln2_g, ln2_b, w_in, hgrn_lb_fwd, hgrn_lb_bwd, hgrn_norm_g, w_out,
                             ln3_g, ln3_b, ffn2_w_gate, ffn2_w_up, ffn2_w_down)
    y_sample = encoder_trunk(x_sample, ln1_g, ln1_b, ffn1_w_gate, ffn1_w_up, ffn1_w_down,
                             ln2_g, ln2_b, w_in, hgrn_lb_fwd, hgrn_lb_bwd, hgrn_norm_g, w_out,
                             ln3_g, ln3_b, ffn2_w_gate, ffn2_w_up, ffn2_w_down)
    return (y_prompt, y_sample)
```

```python
import functools

import numpy as np
import jax
import jax.numpy as jnp
from jax import lax
from jax.experimental import pallas as pl
from jax.experimental.pallas import tpu as pltpu

F32 = jnp.float32
BF16 = jnp.bfloat16

D_MODEL = 2048
D_FF = 5632
DEPTH = 1
HEAD_DIM = 128
N_HEADS_HGRN = 8
N_HEADS_ATTN = 8
HGRN_WIDTH = N_HEADS_HGRN * HEAD_DIM
ATTN_WIDTH = N_HEADS_ATTN * HEAD_DIM
IN_WIDTH = 5 * HGRN_WIDTH + 3 * ATTN_WIDTH
ALPHA = (2 * DEPTH) ** 0.25
LN_EPS = 1e-5
RMS_EPS = 1e-6
NEG_INF = -1e30

CHUNK = 64
HGRN_LEVELS = (1, 2, 4, 8, 16, 32)
N_LEVELS = len(HGRN_LEVELS)

ATTN_HALF = 64
ATTN_DILATIONS = (1, 4, 16)
ATTN_CLASSES = 4
ATTN_BLOCK = 64
ATTN_PAD = 256
ATTN_SAME = ATTN_BLOCK + 2 * ATTN_PAD
ATTN_OTHER_LEAD = 32
ATTN_OTHER = 128

VMEM_LIMIT_BYTES = 56 * 1024 * 1024


def _layer_norm(y, g, b):
    mu = jnp.mean(y, axis=-1, keepdims=True)
    yc = y - mu
    var = jnp.mean(yc * yc, axis=-1, keepdims=True)
    return yc * lax.rsqrt(var + LN_EPS) * g + b


def _dot_nt(a, b):
    return lax.dot_general(a, b, (((1,), (1,)), ((), ())), preferred_element_type=F32)


def _dot_tn(a, b):
    return lax.dot_general(a, b, (((0,), (0,)), ((), ())), preferred_element_type=F32)


def _ffn_kernel(x_ref, wg_ref, wu_ref, wd_ref, g_ref, b_ref, o_ref, xb_ref, acc_ref, *, n_ff_tiles):
    j = pl.program_id(1)

    @pl.when(j == 0)
    def _():
        xb_ref[...] = x_ref[...].astype(BF16)
        acc_ref[...] = jnp.zeros_like(acc_ref)

    xb = xb_ref[...]
    gate = jnp.dot(xb, wg_ref[...], preferred_element_type=F32)
    up = jnp.dot(xb, wu_ref[...], preferred_element_type=F32)
    hidden = (gate * jax.nn.sigmoid(gate) * up).astype(BF16)
    acc_ref[...] += jnp.dot(hidden, wd_ref[...], preferred_element_type=F32)

    @pl.when(j == n_ff_tiles - 1)
    def _():
        y = ALPHA * x_ref[...] + 0.5 * acc_ref[...]
        o_ref[...] = _layer_norm(y, g_ref[...], b_ref[...])


def _ffn(x, wg, wu, wd, g, b, *, tm=512, tf=512):
    t, d = x.shape
    n_ff_tiles = D_FF // tf
    return pl.pallas_call(
        functools.partial(_ffn_kernel, n_ff_tiles=n_ff_tiles),
        grid=(t // tm, n_ff_tiles),
        in_specs=[
            pl.BlockSpec((tm, d), lambda i, j: (i, 0)),
            pl.BlockSpec((d, tf), lambda i, j: (0, j)),
            pl.BlockSpec((d, tf), lambda i, j: (0, j)),
            pl.BlockSpec((tf, d), lambda i, j: (j, 0)),
            pl.BlockSpec((1, d), lambda i, j: (0, 0)),
            pl.BlockSpec((1, d), lambda i, j: (0, 0)),
        ],
        out_specs=pl.BlockSpec((tm, d), lambda i, j: (i, 0)),
        out_shape=jax.ShapeDtypeStruct((t, d), F32),
        scratch_shapes=[pltpu.VMEM((tm, d), BF16), pltpu.VMEM((tm, d), F32)],
        compiler_params=pltpu.CompilerParams(
            dimension_semantics=("parallel", "arbitrary"), vmem_limit_bytes=VMEM_LIMIT_BYTES),
        name="ffn",
    )(x, wg, wu, wd, g, b)


N_HGRN_PARTS = 5
IN_TILE = 1024


def _in_proj_kernel(x_ref, w_ref, hg_ref, qa_ref, ka_ref, va_ref, xb_ref):
    j = pl.program_id(1)

    @pl.when(j == 0)
    def _():
        xb_ref[...] = x_ref[...].astype(BF16)

    r = jnp.dot(xb_ref[...], w_ref[...], preferred_element_type=F32)

    @pl.when(j < N_HGRN_PARTS)
    def _():
        for c in range(N_HEADS_HGRN):
            hg_ref[c] = r[:, c * HEAD_DIM:(c + 1) * HEAD_DIM]

    for part, ref in enumerate((qa_ref, ka_ref, va_ref)):
        @pl.when(j == N_HGRN_PARTS + part)
        def _(ref=ref):
            ref[...] = r.astype(BF16)


def _in_proj(x, w_in, *, tm=512):
    t, d = x.shape
    n_col = IN_WIDTH // IN_TILE
    last_hg = N_HGRN_PARTS - 1
    tok = lambda i, j: (i, 0)
    return pl.pallas_call(
        _in_proj_kernel,
        grid=(t // tm, n_col),
        in_specs=[
            pl.BlockSpec((tm, d), tok),
            pl.BlockSpec((d, IN_TILE), lambda i, j: (0, j)),
        ],
        out_specs=[
            pl.BlockSpec((N_HEADS_HGRN, tm, HEAD_DIM), lambda i, j: (jnp.minimum(j, last_hg), i, 0)),
            pl.BlockSpec((tm, ATTN_WIDTH), tok),
            pl.BlockSpec((tm, ATTN_WIDTH), tok),
            pl.BlockSpec((tm, ATTN_WIDTH), tok),
        ],
        out_shape=[
            jax.ShapeDtypeStruct((N_HGRN_PARTS * N_HEADS_HGRN, t, HEAD_DIM), F32),
            jax.ShapeDtypeStruct((t, ATTN_WIDTH), BF16),
            jax.ShapeDtypeStruct((t, ATTN_WIDTH), BF16),
            jax.ShapeDtypeStruct((t, ATTN_WIDTH), BF16),
        ],
        scratch_shapes=[pltpu.VMEM((tm, d), BF16)],
        compiler_params=pltpu.CompilerParams(
            dimension_semantics=("parallel", "arbitrary"), vmem_limit_bytes=VMEM_LIMIT_BYTES),
        name="in_proj",
    )(x, w_in)


def _hgrn_tables():
    t = np.arange(CHUNK)[:, None]
    u = np.arange(CHUNK)[None, :]
    sums, masks = [], [(t == u)]
    for c in HGRN_LEVELS:
        bound = (t // (2 * c)) * (2 * c) + c - 1
        later = (t % (2 * c)) >= c
        sums.append(np.where(later, (u > bound) & (u <= t), (u > t) & (u <= bound)))
        same = (t // (2 * c)) == (u // (2 * c))
        masks.append(same & later & ((u % (2 * c)) < c))
    sums.append(u <= t)
    sums.append(u > t)
    sums = np.concatenate(sums, axis=0).astype(np.float32)
    masks = np.stack(masks, axis=0).astype(np.float32)
    return sums, masks


def _split3(x):
    a = x.astype(BF16)
    r = x - a.astype(F32)
    b = r.astype(BF16)
    c = (r - b.astype(F32)).astype(BF16)
    return a, b, c


def _hgrn_chunk(q, z, v, lb, sums3, masks, state_ref, last_row):
    f = lb + (1.0 - lb) * jax.nn.sigmoid(z)
    logf = jnp.log(f)
    k = 1.0 - f
    e = jnp.dot(sums3, jnp.concatenate(_split3(logf), axis=0), preferred_element_type=F32)
    n_lv = N_LEVELS * CHUNK
    decay_lv = jnp.exp(e[:n_lv]).reshape(N_LEVELS, CHUNK, HEAD_DIM)
    g_incl = e[n_lv:n_lv + CHUNK]
    decay_in = jnp.exp(g_incl)
    decay_out = jnp.exp(e[n_lv + CHUNK:])
    decay_all = jnp.exp(g_incl[last_row:last_row + 1])

    qb, kb, vb = q.astype(BF16), k.astype(BF16), v.astype(BF16)
    q_lv = (q[None] * decay_lv).astype(BF16)
    k_lv = (k[None] * decay_lv).astype(BF16)
    a = _dot_nt(qb, kb) * masks[0]
    for lv in range(N_LEVELS):
        a = a + _dot_nt(q_lv[lv], k_lv[lv]) * masks[lv + 1]
    state = state_ref[...]
    out = jnp.dot(a.astype(BF16), vb, preferred_element_type=F32)
    out = out + _dot_nt((q * decay_in).astype(BF16), state.astype(BF16))
    state_ref[...] = state * decay_all + _dot_tn(vb, (k * decay_out).astype(BF16))
    return out


def _hgrn_kernel(q_ref, zf_ref, zb_ref, v_ref, g_ref, lbf_ref, lbb_ref, ng_ref,
                 sf_ref, sb_ref, mf_ref, mb_ref, o_ref, of_ref, ob_ref, stf_ref, stb_ref, *, n_chunks):
    stf_ref[...] = jnp.zeros_like(stf_ref)
    stb_ref[...] = jnp.zeros_like(stb_ref)

    def lower_bound(ref):
        p = ref[...]
        p = jnp.exp(p - jnp.max(p, axis=0, keepdims=True))
        return p[0:1] / jnp.sum(p, axis=0, keepdims=True)

    lb_f, lb_b = lower_bound(lbf_ref), lower_bound(lbb_ref)

    def scan_step(n, carry):
        rf = pl.multiple_of(n * CHUNK, CHUNK)
        rb = pl.multiple_of((n_chunks - 1 - n) * CHUNK, CHUNK)
        of_ref[pl.ds(rf, CHUNK), :] = _hgrn_chunk(
            q_ref[pl.ds(rf, CHUNK), :], zf_ref[pl.ds(rf, CHUNK), :], v_ref[pl.ds(rf, CHUNK), :],
            lb_f, sf_ref[...], mf_ref[...], stf_ref, CHUNK - 1)
        ob_ref[pl.ds(rb, CHUNK), :] = _hgrn_chunk(
            q_ref[pl.ds(rb, CHUNK), :], zb_ref[pl.ds(rb, CHUNK), :], v_ref[pl.ds(rb, CHUNK), :],
            lb_b, sb_ref[...], mb_ref[...], stb_ref, 0)
        return carry

    lax.fori_loop(0, n_chunks, scan_step, 0)

    norm_g = ng_ref[...]

    def finish(n, carry):
        r = pl.multiple_of(n * CHUNK, CHUNK)
        o = of_ref[pl.ds(r, CHUNK), :] + ob_ref[pl.ds(r, CHUNK), :]
        o = o * lax.rsqrt(jnp.mean(o * o, axis=-1, keepdims=True) + RMS_EPS)
        gate = g_ref[pl.ds(r, CHUNK), :]
        o_ref[pl.ds(r, CHUNK), :] = (o * norm_g * (gate * jax.nn.sigmoid(gate))).astype(BF16)
        return carry

    lax.fori_loop(0, n_chunks, finish, 0)


def _hgrn(hg, lb_fwd, lb_bwd, norm_g, batch, seq):
    hg5 = hg.reshape(N_HGRN_PARTS, N_HEADS_HGRN, batch, seq, HEAD_DIM)
    sums, masks = _hgrn_tables()
    sums3_f = jnp.asarray(np.tile(sums, (1, 3)), BF16)
    sums3_b = jnp.asarray(np.tile(sums.reshape(-1, CHUNK, CHUNK)[:, ::-1, ::-1].reshape(sums.shape), (1, 3)), BF16)
    masks_f = jnp.asarray(masks, F32)
    masks_b = jnp.asarray(masks[:, ::-1, ::-1], F32)

    def part(p):
        return pl.BlockSpec((None, None, None, seq, HEAD_DIM), lambda b, h: (p, h, b, 0, 0))

    def lb_spec():
        return pl.BlockSpec((None, DEPTH + 1, HEAD_DIM), lambda b, h: (h, 0, 0))

    def lb_per_head(lb):
        return lb.reshape(DEPTH + 1, N_HEADS_HGRN, HEAD_DIM).transpose(1, 0, 2)

    const2 = lambda b, h: (0, 0)
    const3 = lambda b, h: (0, 0, 0)
    n_sum_rows = sums.shape[0]
    return pl.pallas_call(
        functools.partial(_hgrn_kernel, n_chunks=seq // CHUNK),
        grid=(batch, N_HEADS_HGRN),
        in_specs=[
            part(0), part(1), part(2), part(3), part(4),
            lb_spec(), lb_spec(),
            pl.BlockSpec((None, 1, HEAD_DIM), lambda b, h: (h, 0, 0)),
            pl.BlockSpec((n_sum_rows, 3 * CHUNK), const2),
            pl.BlockSpec((n_sum_rows, 3 * CHUNK), const2),
            pl.BlockSpec((N_LEVELS + 1, CHUNK, CHUNK), const3),
            pl.BlockSpec((N_LEVELS + 1, CHUNK, CHUNK), const3),
        ],
        out_specs=pl.BlockSpec((None, seq, HEAD_DIM), lambda b, h: (b, 0, h)),
        out_shape=jax.ShapeDtypeStruct((batch, seq, HGRN_WIDTH), BF16),
        scratch_shapes=[
            pltpu.VMEM((seq, HEAD_DIM), F32), pltpu.VMEM((seq, HEAD_DIM), F32),
            pltpu.VMEM((HEAD_DIM, HEAD_DIM), F32), pltpu.VMEM((HEAD_DIM, HEAD_DIM), F32),
        ],
        compiler_params=pltpu.CompilerParams(
            dimension_semantics=("parallel", "parallel"), vmem_limit_bytes=VMEM_LIMIT_BYTES),
        name="hgrn",
    )(hg5, hg5, hg5, hg5, hg5,
      lb_per_head(lb_fwd), lb_per_head(lb_bwd), norm_g.reshape(N_HEADS_HGRN, 1, HEAD_DIM), sums3_f, sums3_b, masks_f, masks_b)


def _attn_block_types(n_rows):
    row = np.arange(ATTN_BLOCK)

    def flags(n):
        i = n * ATTN_BLOCK + row
        reach = [ATTN_HALF * d // ATTN_CLASSES for d in ATTN_DILATIONS]
        return np.stack([i < r for r in reach] + [i >= n_rows - r for r in reach]).tobytes()

    starts = [0]
    for n in range(1, n_rows // ATTN_BLOCK):
        if flags(n) != flags(n - 1):
            starts.append(n)
    return tuple(starts)


def _attn_bias(delta, i_query, slope, n_rows):
    mult = jnp.zeros(delta.shape, F32)
    for d in ATTN_DILATIONS:
        reach = ATTN_HALF * d // ATTN_CLASSES
        lo = jnp.where(i_query < reach, 0, -ATTN_HALF * d)
        hi = jnp.where(i_query >= n_rows - reach, (ATTN_HALF - 1) * d, ATTN_HALF * d)
        hit = (delta >= lo) & (delta <= hi)
        if d > 1:
            hit = hit & (lax.rem(delta, d) == 0)
        mult = mult + hit.astype(F32)
    dist = jnp.abs(delta).astype(F32)
    return jnp.where(mult > 0.0, jnp.log(jnp.maximum(mult, 1.0)) - slope * dist, NEG_INF)


def _attn_kernel(slope_ref, q_ref, k0_ref, k1_ref, k2_ref, k3_ref, v0_ref, v1_ref, v2_ref, v3_ref,
                 o_ref, kpad_ref, vpad_ref, bias_same_ref, bias_other_ref, *, n_rows, type_starts):
    rq = pl.program_id(2)
    slope = slope_ref[...][:, 0:1]

    @pl.when(rq == 0)
    def _():
        zeros = jnp.zeros((ATTN_PAD, HEAD_DIM), BF16)
        for c, (k_ref, v_ref) in enumerate(((k0_ref, v0_ref), (k1_ref, v1_ref), (k2_ref, v2_ref), (k3_ref, v3_ref))):
            for pad_ref, src_ref in ((kpad_ref, k_ref), (vpad_ref, v_ref)):
                pad_ref[c, 0:ATTN_PAD, :] = zeros
                pad_ref[c, ATTN_PAD:ATTN_PAD + n_rows, :] = src_ref[...]
                pad_ref[c, ATTN_PAD + n_rows:ATTN_PAD + n_rows + ATTN_PAD, :] = zeros
        row = lax.broadcasted_iota(jnp.int32, (ATTN_BLOCK, ATTN_SAME), 0)
        col = lax.broadcasted_iota(jnp.int32, (ATTN_BLOCK, ATTN_SAME), 1)
        for ty, first_block in enumerate(type_starts):
            bias_same_ref[ty] = _attn_bias((col - ATTN_PAD - row) * ATTN_CLASSES,
                                           first_block * ATTN_BLOCK + row, slope, n_rows)

    row = lax.broadcasted_iota(jnp.int32, (ATTN_BLOCK, ATTN_OTHER), 0)
    col = lax.broadcasted_iota(jnp.int32, (ATTN_BLOCK, ATTN_OTHER), 1)
    for m in range(1, ATTN_CLASSES):
        rk = lax.rem(rq + m, ATTN_CLASSES)
        delta = (col - ATTN_OTHER_LEAD - row) * ATTN_CLASSES + (rk - rq)
        for ty, first_block in enumerate(type_starts):
            bias_other_ref[ty, m - 1] = _attn_bias(delta, first_block * ATTN_BLOCK + row, slope, n_rows)

    scale = HEAD_DIM ** -0.5

    def block(n, carry):
        i0 = pl.multiple_of(n * ATTN_BLOCK, ATTN_BLOCK)
        ty = jnp.int32(0)
        for first_block in type_starts[1:]:
            ty = ty + (n >= first_block).astype(jnp.int32)
        qb = q_ref[pl.ds(i0, ATTN_BLOCK), :]
        scores, values = [], []
        s = _dot_nt(qb, kpad_ref[rq, pl.ds(i0, ATTN_SAME), :]) * scale
        scores.append(s + bias_same_ref[ty])
        values.append(vpad_ref[rq, pl.ds(i0, ATTN_SAME), :])
        start = pl.multiple_of(i0 + (ATTN_PAD - ATTN_OTHER_LEAD), ATTN_OTHER_LEAD)
        for m in range(1, ATTN_CLASSES):
            rk = lax.rem(rq + m, ATTN_CLASSES)
            s = _dot_nt(qb, kpad_ref[rk, pl.ds(start, ATTN_OTHER), :]) * scale
            scores.append(s + bias_other_ref[ty, m - 1])
            values.append(vpad_ref[rk, pl.ds(start, ATTN_OTHER), :])
        mx = scores[0].max(axis=-1, keepdims=True)
        for s in scores[1:]:
            mx = jnp.maximum(mx, s.max(axis=-1, keepdims=True))
        den = jnp.zeros((ATTN_BLOCK, 1), F32)
        acc = jnp.zeros((ATTN_BLOCK, HEAD_DIM), F32)
        for s, v in zip(scores, values):
            p = jnp.exp(s - mx)
            den = den + p.sum(axis=-1, keepdims=True)
            acc = acc + jnp.dot(p.astype(BF16), v, preferred_element_type=F32)
        o_ref[pl.ds(i0, ATTN_BLOCK), :] = (acc / den).astype(BF16)
        return carry

    lax.fori_loop(0, n_rows // ATTN_BLOCK, block, 0)


def _attn(qa, ka, va, batch, seq):
    n_rows = seq // ATTN_CLASSES
    assert seq % (ATTN_BLOCK * max(ATTN_DILATIONS)) == 0 and seq >= 2 * ATTN_BLOCK * max(ATTN_DILATIONS)
    type_starts = _attn_block_types(n_rows)
    n_types = len(type_starts)
    view = lambda a: a.reshape(batch, n_rows, ATTN_CLASSES * ATTN_WIDTH)
    slopes = 2.0 ** (-8.0 * (jnp.arange(N_HEADS_ATTN, dtype=F32) + 1.0) / N_HEADS_ATTN)
    slopes = jnp.broadcast_to(slopes[:, None, None], (N_HEADS_ATTN, 1, HEAD_DIM))

    def cls(c):
        return pl.BlockSpec((None, n_rows, HEAD_DIM), lambda b, h, r: (b, 0, c * N_HEADS_ATTN + h))

    own = pl.BlockSpec((None, n_rows, HEAD_DIM), lambda b, h, r: (b, 0, r * N_HEADS_ATTN + h))
    padded = n_rows + 2 * ATTN_PAD
    q4, k4, v4 = view(qa), view(ka), view(va)
    out = pl.pallas_call(
        functools.partial(_attn_kernel, n_rows=n_rows, type_starts=type_starts),
        grid=(batch, N_HEADS_ATTN, ATTN_CLASSES),
        in_specs=[pl.BlockSpec((None, 1, HEAD_DIM), lambda b, h, r: (h, 0, 0)), own,
                  cls(0), cls(1), cls(2), cls(3), cls(0), cls(1), cls(2), cls(3)],
        out_specs=own,
        out_shape=jax.ShapeDtypeStruct((batch, n_rows, ATTN_CLASSES * ATTN_WIDTH), BF16),
        scratch_shapes=[
            pltpu.VMEM((ATTN_CLASSES, padded, HEAD_DIM), BF16),
            pltpu.VMEM((ATTN_CLASSES, padded, HEAD_DIM), BF16),
            pltpu.VMEM((n_types, ATTN_BLOCK, ATTN_SAME), F32),
            pltpu.VMEM((n_types, ATTN_CLASSES - 1, ATTN_BLOCK, ATTN_OTHER), F32),
        ],
        compiler_params=pltpu.CompilerParams(
            dimension_semantics=("parallel", "parallel", "arbitrary"), vmem_limit_bytes=VMEM_LIMIT_BYTES),
        name="attn",
    )(slopes, q4, k4, k4, k4, k4, v4, v4, v4, v4)
    return out.reshape(batch * seq, ATTN_WIDTH)


def _out_proj_kernel(hg_ref, at_ref, w_ref, x_ref, g_ref, b_ref, o_ref):
    mix = jnp.dot(hg_ref[...], w_ref[0:HGRN_WIDTH, :], preferred_element_type=F32)
    mix = mix + jnp.dot(at_ref[...], w_ref[HGRN_WIDTH:HGRN_WIDTH + ATTN_WIDTH, :], preferred_element_type=F32)
    o_ref[...] = _layer_norm(ALPHA * x_ref[...] + mix, g_ref[...], b_ref[...])


def _out_proj(hg, at, w_out, x, g, b, *, tm=512):
    t, d = x.shape
    tok = lambda i: (i, 0)
    const = lambda i: (0, 0)
    return pl.pallas_call(
        _out_proj_kernel,
        grid=(t // tm,),
        in_specs=[
            pl.BlockSpec((tm, HGRN_WIDTH), tok),
            pl.BlockSpec((tm, ATTN_WIDTH), tok),
            pl.BlockSpec((HGRN_WIDTH + ATTN_WIDTH, d), const),
            pl.BlockSpec((tm, d), tok),
            pl.BlockSpec((1, d), const),
            pl.BlockSpec((1, d), const),
        ],
        out_specs=pl.BlockSpec((tm, d), tok),
        out_shape=jax.ShapeDtypeStruct((t, d), F32),
        compiler_params=pltpu.CompilerParams(
            dimension_semantics=("parallel",), vmem_limit_bytes=VMEM_LIMIT_BYTES),
        name="out_proj",
    )(hg, at, w_out, x, g, b)


def _encoder_layer(x, w, layer):
    batch, seq, d = x.shape
    h = x.reshape(batch * seq, d)
    h = _ffn(h, w["ffn1_w_gate"][layer], w["ffn1_w_up"][layer], w["ffn1_w_down"][layer],
             w["ln1_g"][layer:layer + 1], w["ln1_b"][layer:layer + 1])
    hg, qa, ka, va = _in_proj(h, w["w_in"][layer])
    hgrn_out = _hgrn(hg, w["hgrn_lb_fwd"], w["hgrn_lb_bwd"], w["hgrn_norm_g"][layer], batch, seq)
    attn_out = _attn(qa, ka, va, batch, seq)
    h = _out_proj(hgrn_out.reshape(batch * seq, HGRN_WIDTH), attn_out, w["w_out"][layer], h,
                  w["ln2_g"][layer:layer + 1], w["ln2_b"][layer:layer + 1])
    h = _ffn(h, w["ffn2_w_gate"][layer], w["ffn2_w_up"][layer], w["ffn2_w_down"][layer],
             w["ln3_g"][layer:layer + 1], w["ln3_b"][layer:layer + 1])
    return h.reshape(batch, seq, d)


_MATMUL_WEIGHTS = ("ffn1_w_gate", "ffn1_w_up", "ffn1_w_down", "w_in", "w_out",
                   "ffn2_w_gate", "ffn2_w_up", "ffn2_w_down")


def kernel(x_prompt, x_sample, ln1_g, ln1_b, ffn1_w_gate, ffn1_w_up, ffn1_w_down, ln2_g, ln2_b, w_in,
           hgrn_lb_fwd, hgrn_lb_bwd, hgrn_norm_g, w_out, ln3_g, ln3_b, ffn2_w_gate, ffn2_w_up, ffn2_w_down):
    w = dict(ln1_g=ln1_g, ln1_b=ln1_b, ffn1_w_gate=ffn1_w_gate, ffn1_w_up=ffn1_w_up, ffn1_w_down=ffn1_w_down,
             ln2_g=ln2_g, ln2_b=ln2_b, w_in=w_in, hgrn_lb_fwd=hgrn_lb_fwd, hgrn_lb_bwd=hgrn_lb_bwd,
             hgrn_norm_g=hgrn_norm_g, w_out=w_out, ln3_g=ln3_g, ln3_b=ln3_b,
             ffn2_w_gate=ffn2_w_gate, ffn2_w_up=ffn2_w_up, ffn2_w_down=ffn2_w_down)
    for name in _MATMUL_WEIGHTS:
        w[name] = w[name].astype(BF16)
    outs = []
    for x in (x_prompt, x_sample):
        for layer in range(DEPTH):
            x = _encoder_layer(x, w, layer)
        outs.append(x)
    return tuple(outs)
```

```python
import functools

import numpy as np
import jax
import jax.numpy as jnp
from jax import lax
from jax.experimental import pallas as pl
from jax.experimental.pallas import tpu as pltpu

F32 = jnp.float32
BF16 = jnp.bfloat16

D_MODEL = 2048
D_FF = 5632
DEPTH = 1
HEAD_DIM = 128
N_HEADS_HGRN = 8
N_HEADS_ATTN = 8
HGRN_WIDTH = N_HEADS_HGRN * HEAD_DIM
ATTN_WIDTH = N_HEADS_ATTN * HEAD_DIM
IN_WIDTH = 5 * HGRN_WIDTH + 3 * ATTN_WIDTH
ALPHA = (2 * DEPTH) ** 0.25
LN_EPS = 1e-5
RMS_EPS = 1e-6
NEG_INF = -1e30

CHUNK = 64
HGRN_LEVELS = (1, 2, 4, 8, 16, 32)
N_LEVELS = len(HGRN_LEVELS)

ATTN_HALF = 64
ATTN_DILATIONS = (1, 4, 16)
ATTN_CLASSES = 4
ATTN_BLOCK = 64
ATTN_PAD = 256
ATTN_SAME = ATTN_BLOCK + 2 * ATTN_PAD
ATTN_OTHER_LEAD = 32
ATTN_OTHER = 128
ATTN_UNROLL = 4

VMEM_LIMIT_BYTES = 56 * 1024 * 1024


def _layer_norm(y, g, b):
    mu = jnp.mean(y, axis=-1, keepdims=True)
    yc = y - mu
    var = jnp.mean(yc * yc, axis=-1, keepdims=True)
    return yc * lax.rsqrt(var + LN_EPS) * g + b


def _dot_nt(a, b):
    return lax.dot_general(a, b, (((1,), (1,)), ((), ())), preferred_element_type=F32)


def _dot_tn(a, b):
    return lax.dot_general(a, b, (((0,), (0,)), ((), ())), preferred_element_type=F32)


def _ffn_kernel(x_ref, wg_ref, wu_ref, wd_ref, g_ref, b_ref, o_ref, xb_ref, acc_ref, *, n_ff_tiles):
    j = pl.program_id(1)

    @pl.when(j == 0)
    def _():
        xb_ref[...] = x_ref[...].astype(BF16)
        acc_ref[...] = jnp.zeros_like(acc_ref)

    xb = xb_ref[...]
    gate = jnp.dot(xb, wg_ref[...], preferred_element_type=F32)
    up = jnp.dot(xb, wu_ref[...], preferred_element_type=F32)
    hidden = (gate * jax.nn.sigmoid(gate) * up).astype(BF16)
    acc_ref[...] += jnp.dot(hidden, wd_ref[...], preferred_element_type=F32)

    @pl.when(j == n_ff_tiles - 1)
    def _():
        y = ALPHA * x_ref[...] + 0.5 * acc_ref[...]
        o_ref[...] = _layer_norm(y, g_ref[...], b_ref[...])


def _ffn(x, wg, wu, wd, g, b, *, tm=512, tf=512):
    t, d = x.shape
    n_ff_tiles = D_FF // tf
    return pl.pallas_call(
        functools.partial(_ffn_kernel, n_ff_tiles=n_ff_tiles),
        grid=(t // tm, n_ff_tiles),
        in_specs=[
            pl.BlockSpec((tm, d), lambda i, j: (i, 0)),
            pl.BlockSpec((d, tf), lambda i, j: (0, j)),
            pl.BlockSpec((d, tf), lambda i, j: (0, j)),
            pl.BlockSpec((tf, d), lambda i, j: (j, 0)),
            pl.BlockSpec((1, d), lambda i, j: (0, 0)),
            pl.BlockSpec((1, d), lambda i, j: (0, 0)),
        ],
        out_specs=pl.BlockSpec((tm, d), lambda i, j: (i, 0)),
        out_shape=jax.ShapeDtypeStruct((t, d), F32),
        scratch_shapes=[pltpu.VMEM((tm, d), BF16), pltpu.VMEM((tm, d), F32)],
        compiler_params=pltpu.CompilerParams(
            dimension_semantics=("parallel", "arbitrary"), vmem_limit_bytes=VMEM_LIMIT_BYTES),
        name="ffn",
    )(x, wg, wu, wd, g, b)


N_HGRN_PARTS = 5
IN_TILE = 1024


def _in_proj_kernel(x_ref, w_ref, hg_ref, qa_ref, ka_ref, va_ref, xb_ref):
    j = pl.program_id(1)

    @pl.when(j == 0)
    def _():
        xb_ref[...] = x_ref[...].astype(BF16)

    r = jnp.dot(xb_ref[...], w_ref[...], preferred_element_type=F32)

    @pl.when(j < N_HGRN_PARTS)
    def _():
        for c in range(N_HEADS_HGRN):
            hg_ref[c] = r[:, c * HEAD_DIM:(c + 1) * HEAD_DIM]

    for part, ref in enumerate((qa_ref, ka_ref, va_ref)):
        @pl.when(j == N_HGRN_PARTS + part)
        def _(ref=ref):
            ref[...] = r.astype(BF16)


def _in_proj(x, w_in, *, tm=512):
    t, d = x.shape
    n_col = IN_WIDTH // IN_TILE
    last_hg = N_HGRN_PARTS - 1
    tok = lambda i, j: (i, 0)
    return pl.pallas_call(
        _in_proj_kernel,
        grid=(t // tm, n_col),
        in_specs=[
            pl.BlockSpec((tm, d), tok),
            pl.BlockSpec((d, IN_TILE), lambda i, j: (0, j)),
        ],
        out_specs=[
            pl.BlockSpec((N_HEADS_HGRN, tm, HEAD_DIM), lambda i, j: (jnp.minimum(j, last_hg), i, 0)),
            pl.BlockSpec((tm, ATTN_WIDTH), tok),
            pl.BlockSpec((tm, ATTN_WIDTH), tok),
            pl.BlockSpec((tm, ATTN_WIDTH), tok),
        ],
        out_shape=[
            jax.ShapeDtypeStruct((N_HGRN_PARTS * N_HEADS_HGRN, t, HEAD_DIM), F32),
            jax.ShapeDtypeStruct((t, ATTN_WIDTH), BF16),
            jax.ShapeDtypeStruct((t, ATTN_WIDTH), BF16),
            jax.ShapeDtypeStruct((t, ATTN_WIDTH), BF16),
        ],
        scratch_shapes=[pltpu.VMEM((tm, d), BF16)],
        compiler_params=pltpu.CompilerParams(
            dimension_semantics=("parallel", "arbitrary"), vmem_limit_bytes=VMEM_LIMIT_BYTES),
        name="in_proj",
    )(x, w_in)


def _hgrn_tables():
    t = np.arange(CHUNK)[:, None]
    u = np.arange(CHUNK)[None, :]
    sums, masks = [], [(t == u)]
    for c in HGRN_LEVELS:
        bound = (t // (2 * c)) * (2 * c) + c - 1
        later = (t % (2 * c)) >= c
        sums.append(np.where(later, (u > bound) & (u <= t), (u > t) & (u <= bound)))
        same = (t // (2 * c)) == (u // (2 * c))
        masks.append(same & later & ((u % (2 * c)) < c))
    sums.append(u <= t)
    sums.append(u > t)
    sums = np.concatenate(sums, axis=0).astype(np.float32)
    masks = np.stack(masks, axis=0).astype(np.float32)
    return sums, masks


def _split3(x):
    a = x.astype(BF16)
    r = x - a.astype(F32)
    b = r.astype(BF16)
    c = (r - b.astype(F32)).astype(BF16)
    return a, b, c


def _hgrn_chunk(q, z, v, lb, sums3, masks, state_ref, last_row):
    f = lb + (1.0 - lb) * jax.nn.sigmoid(z)
    logf = jnp.log(f)
    k = 1.0 - f
    e = jnp.dot(sums3, jnp.concatenate(_split3(logf), axis=0), preferred_element_type=F32)
    n_lv = N_LEVELS * CHUNK
    decay_lv = jnp.exp(e[:n_lv]).reshape(N_LEVELS, CHUNK, HEAD_DIM)
    g_incl = e[n_lv:n_lv + CHUNK]
    decay_in = jnp.exp(g_incl)
    decay_out = jnp.exp(e[n_lv + CHUNK:])
    decay_all = jnp.exp(g_incl[last_row:last_row + 1])

    qb, kb, vb = q.astype(BF16), k.astype(BF16), v.astype(BF16)
    q_lv = (q[None] * decay_lv).astype(BF16)
    k_lv = (k[None] * decay_lv).astype(BF16)
    a = _dot_nt(qb, kb) * masks[0]
    for lv in range(N_LEVELS):
        a = a + _dot_nt(q_lv[lv], k_lv[lv]) * masks[lv + 1]
    state = state_ref[...]
    out = jnp.dot(a.astype(BF16), vb, preferred_element_type=F32)
    out = out + _dot_nt((q * decay_in).astype(BF16), state.astype(BF16))
    state_ref[...] = state * decay_all + _dot_tn(vb, (k * decay_out).astype(BF16))
    return out


def _hgrn_kernel(qf_ref, zf_ref, vf_ref, gf_ref, qb_ref, zb_ref, vb_ref, gb_ref, lbf_ref, lbb_ref, ng_ref,
                 sf_ref, sb_ref, mf_ref, mb_ref, o_ref, part_ref, stf_ref, stb_ref, *, n_tiles, tile):
    s = pl.program_id(2)
    n_heads = qf_ref.shape[0]
    chunks = tile // CHUNK

    @pl.when(s == 0)
    def _():
        stf_ref[...] = jnp.zeros_like(stf_ref)
        stb_ref[...] = jnp.zeros_like(stb_ref)

    def lower_bound(ref, h):
        p = ref[h]
        p = jnp.exp(p - jnp.max(p, axis=0, keepdims=True))
        return p[0:1] / jnp.sum(p, axis=0, keepdims=True)

    def scan(finish):
        def scan_step(c, carry):
            lf = pl.multiple_of(c * CHUNK, CHUNK)
            lb = pl.multiple_of((chunks - 1 - c) * CHUNK, CHUNK)
            rows_f = pl.ds(lf, CHUNK)
            rows_b = pl.ds(lb, CHUNK)
            out_f = pl.ds(pl.multiple_of(s * tile + lf, CHUNK), CHUNK)
            out_b = pl.ds(pl.multiple_of((n_tiles - 1 - s) * tile + lb, CHUNK), CHUNK)
            res = []
            for h in range(n_heads):
                res.append(_hgrn_chunk(qf_ref[h, rows_f, :], zf_ref[h, rows_f, :], vf_ref[h, rows_f, :],
                                       lower_bound(lbf_ref, h), sf_ref[...], mf_ref[...], stf_ref.at[h], CHUNK - 1))
                res.append(_hgrn_chunk(qb_ref[h, rows_b, :], zb_ref[h, rows_b, :], vb_ref[h, rows_b, :],
                                       lower_bound(lbb_ref, h), sb_ref[...], mb_ref[...], stb_ref.at[h], 0))
            for h in range(n_heads):
                for o, rows, out_rows, g_ref in ((res[2 * h], rows_f, out_f, gf_ref),
                                                 (res[2 * h + 1], rows_b, out_b, gb_ref)):
                    if not finish:
                        part_ref[h, out_rows, :] = o
                        continue
                    o = o + part_ref[h, out_rows, :]
                    o = o * lax.rsqrt(jnp.mean(o * o, axis=-1, keepdims=True) + RMS_EPS)
                    gate = g_ref[h, rows, :]
                    o = o * ng_ref[h] * (gate * jax.nn.sigmoid(gate))
                    o_ref[out_rows, h * HEAD_DIM:(h + 1) * HEAD_DIM] = o.astype(BF16)
            return carry

        lax.fori_loop(0, chunks, scan_step, 0)

    @pl.when(s < n_tiles // 2)
    def _():
        scan(finish=False)

    @pl.when(s >= n_tiles // 2)
    def _():
        scan(finish=True)


HGRN_HEADS_PER_STEP = 4
HGRN_TILE = 512


def _hgrn(hg, lb_fwd, lb_bwd, norm_g, batch, seq):
    hg5 = hg.reshape(N_HGRN_PARTS, N_HEADS_HGRN, batch, seq, HEAD_DIM)
    sums, masks = _hgrn_tables()
    sums3_f = jnp.asarray(np.tile(sums, (1, 3)), BF16)
    sums3_b = jnp.asarray(np.tile(sums.reshape(-1, CHUNK, CHUNK)[:, ::-1, ::-1].reshape(sums.shape), (1, 3)), BF16)
    masks_f = jnp.asarray(masks, F32)
    masks_b = jnp.asarray(masks[:, ::-1, ::-1], F32)
    hps, tile = HGRN_HEADS_PER_STEP, HGRN_TILE
    n_tiles = seq // tile
    assert seq % tile == 0 and n_tiles % 2 == 0 and N_HEADS_HGRN % hps == 0

    def part(p, backward):
        def index(b, hg_, s):
            return (p, hg_, b, (n_tiles - 1 - s) if backward else s, 0)
        return pl.BlockSpec((None, hps, None, tile, HEAD_DIM), index)

    def per_head(rows):
        return pl.BlockSpec((hps, rows, HEAD_DIM), lambda b, hg_, s: (hg_, 0, 0))

    def lb_per_head(lb):
        return lb.reshape(DEPTH + 1, N_HEADS_HGRN, HEAD_DIM).transpose(1, 0, 2)

    const2 = lambda b, hg_, s: (0, 0)
    const3 = lambda b, hg_, s: (0, 0, 0)
    n_sum_rows = sums.shape[0]
    q, zf, zb, v, g = range(N_HGRN_PARTS)
    return pl.pallas_call(
        functools.partial(_hgrn_kernel, n_tiles=n_tiles, tile=tile),
        grid=(batch, N_HEADS_HGRN // hps, n_tiles),
        in_specs=[
            part(q, False), part(zf, False), part(v, False), part(g, False),
            part(q, True), part(zb, True), part(v, True), part(g, True),
            per_head(DEPTH + 1), per_head(DEPTH + 1), per_head(1),
            pl.BlockSpec((n_sum_rows, 3 * CHUNK), const2),
            pl.BlockSpec((n_sum_rows, 3 * CHUNK), const2),
            pl.BlockSpec((N_LEVELS + 1, CHUNK, CHUNK), const3),
            pl.BlockSpec((N_LEVELS + 1, CHUNK, CHUNK), const3),
        ],
        out_specs=pl.BlockSpec((None, seq, hps * HEAD_DIM), lambda b, hg_, s: (b, 0, hg_)),
        out_shape=jax.ShapeDtypeStruct((batch, seq, HGRN_WIDTH), BF16),
        scratch_shapes=[
            pltpu.VMEM((hps, seq, HEAD_DIM), F32),
            pltpu.VMEM((hps, HEAD_DIM, HEAD_DIM), F32), pltpu.VMEM((hps, HEAD_DIM, HEAD_DIM), F32),
        ],
        compiler_params=pltpu.CompilerParams(
            dimension_semantics=("parallel", "parallel", "arbitrary"), vmem_limit_bytes=VMEM_LIMIT_BYTES),
        name="hgrn",
    )(hg5, hg5, hg5, hg5, hg5, hg5, hg5, hg5,
      lb_per_head(lb_fwd), lb_per_head(lb_bwd), norm_g.reshape(N_HEADS_HGRN, 1, HEAD_DIM),
      sums3_f, sums3_b, masks_f, masks_b)


def _attn_block_types(n_rows):
    row = np.arange(ATTN_BLOCK)

    def flags(n):
        i = n * ATTN_BLOCK + row
        reach = [ATTN_HALF * d // ATTN_CLASSES for d in ATTN_DILATIONS]
        return np.stack([i < r for r in reach] + [i >= n_rows - r for r in reach]).tobytes()

    starts = [0]
    for n in range(1, n_rows // ATTN_BLOCK):
        if flags(n) != flags(n - 1):
            starts.append(n)
    return tuple(starts)


def _attn_bias(delta, i_query, slope, n_rows):
    mult = jnp.zeros(delta.shape, F32)
    for d in ATTN_DILATIONS:
        reach = ATTN_HALF * d // ATTN_CLASSES
        lo = jnp.where(i_query < reach, 0, -ATTN_HALF * d)
        hi = jnp.where(i_query >= n_rows - reach, (ATTN_HALF - 1) * d, ATTN_HALF * d)
        hit = (delta >= lo) & (delta <= hi)
        if d > 1:
            hit = hit & (lax.rem(delta, d) == 0)
        mult = mult + hit.astype(F32)
    dist = jnp.abs(delta).astype(F32)
    return jnp.where(mult > 0.0, jnp.log(jnp.maximum(mult, 1.0)) - slope * dist, NEG_INF)


def _attn_kernel(slope_ref, q_ref, k0_ref, k1_ref, k2_ref, k3_ref, v0_ref, v1_ref, v2_ref, v3_ref,
                 o_ref, kpad_ref, vpad_ref, bias_same_ref, bias_other_ref, *, n_rows, type_starts):
    b, rq = pl.program_id(1), pl.program_id(2)

    @pl.when((b == 0) & (rq == 0))
    def _():
        slope = slope_ref[...][:, 0:1]
        row = lax.broadcasted_iota(jnp.int32, (ATTN_BLOCK, ATTN_SAME), 0)
        col = lax.broadcasted_iota(jnp.int32, (ATTN_BLOCK, ATTN_SAME), 1)
        for ty, first_block in enumerate(type_starts):
            bias_same_ref[ty] = _attn_bias((col - ATTN_PAD - row) * ATTN_CLASSES,
                                           first_block * ATTN_BLOCK + row, slope, n_rows)
        row = lax.broadcasted_iota(jnp.int32, (ATTN_BLOCK, ATTN_OTHER), 0)
        col = lax.broadcasted_iota(jnp.int32, (ATTN_BLOCK, ATTN_OTHER), 1)
        for query_class in range(ATTN_CLASSES):
            for m in range(1, ATTN_CLASSES):
                delta = (col - ATTN_OTHER_LEAD - row) * ATTN_CLASSES + ((query_class + m) % ATTN_CLASSES - query_class)
                for ty, first_block in enumerate(type_starts):
                    bias_other_ref[query_class, ty, m - 1] = _attn_bias(
                        delta, first_block * ATTN_BLOCK + row, slope, n_rows)

    @pl.when(rq == 0)
    def _():
        zeros = jnp.zeros((ATTN_PAD, HEAD_DIM), BF16)
        for c, (k_ref, v_ref) in enumerate(((k0_ref, v0_ref), (k1_ref, v1_ref), (k2_ref, v2_ref), (k3_ref, v3_ref))):
            for pad_ref, src_ref in ((kpad_ref, k_ref), (vpad_ref, v_ref)):
                pad_ref[c, 0:ATTN_PAD, :] = zeros
                pad_ref[c, ATTN_PAD:ATTN_PAD + n_rows, :] = src_ref[...]
                pad_ref[c, ATTN_PAD + n_rows:ATTN_PAD + n_rows + ATTN_PAD, :] = zeros

    scale = HEAD_DIM ** -0.5

    def block(n, carry):
        i0 = pl.multiple_of(n * ATTN_BLOCK, ATTN_BLOCK)
        ty = jnp.int32(0)
        for first_block in type_starts[1:]:
            ty = ty + (n >= first_block).astype(jnp.int32)
        qb = q_ref[pl.ds(i0, ATTN_BLOCK), :]
        scores, values = [], []
        s = _dot_nt(qb, kpad_ref[rq, pl.ds(i0, ATTN_SAME), :]) * scale
        scores.append(s + bias_same_ref[ty])
        values.append(vpad_ref[rq, pl.ds(i0, ATTN_SAME), :])
        start = pl.multiple_of(i0 + (ATTN_PAD - ATTN_OTHER_LEAD), ATTN_OTHER_LEAD)
        for m in range(1, ATTN_CLASSES):
            rk = lax.rem(rq + m, ATTN_CLASSES)
            s = _dot_nt(qb, kpad_ref[rk, pl.ds(start, ATTN_OTHER), :]) * scale
            scores.append(s + bias_other_ref[rq, ty, m - 1])
            values.append(vpad_ref[rk, pl.ds(start, ATTN_OTHER), :])
        mx = scores[0].max(axis=-1, keepdims=True)
        for s in scores[1:]:
            mx = jnp.maximum(mx, s.max(axis=-1, keepdims=True))
        den = jnp.zeros((ATTN_BLOCK, 1), F32)
        acc = jnp.zeros((ATTN_BLOCK, HEAD_DIM), F32)
        for s, v in zip(scores, values):
            p = jnp.exp(s - mx)
            den = den + p.sum(axis=-1, keepdims=True)
            acc = acc + jnp.dot(p.astype(BF16), v, preferred_element_type=F32)
        o_ref[pl.ds(i0, ATTN_BLOCK), :] = (acc / den).astype(BF16)
        return carry

    lax.fori_loop(0, n_rows // ATTN_BLOCK, block, 0, unroll=ATTN_UNROLL)


def _attn(qa, ka, va, batch, seq):
    n_rows = seq // ATTN_CLASSES
    assert seq % (ATTN_BLOCK * max(ATTN_DILATIONS)) == 0 and seq >= 2 * ATTN_BLOCK * max(ATTN_DILATIONS)
    type_starts = _attn_block_types(n_rows)
    n_types = len(type_starts)
    view = lambda a: a.reshape(batch, n_rows, ATTN_CLASSES * ATTN_WIDTH)
    slopes = 2.0 ** (-8.0 * (jnp.arange(N_HEADS_ATTN, dtype=F32) + 1.0) / N_HEADS_ATTN)
    slopes = jnp.broadcast_to(slopes[:, None, None], (N_HEADS_ATTN, 1, HEAD_DIM))

    def cls(c):
        return pl.BlockSpec((None, n_rows, HEAD_DIM), lambda h, b, r: (b, 0, c * N_HEADS_ATTN + h))

    own = pl.BlockSpec((None, n_rows, HEAD_DIM), lambda h, b, r: (b, 0, r * N_HEADS_ATTN + h))
    padded = n_rows + 2 * ATTN_PAD
    q4, k4, v4 = view(qa), view(ka), view(va)
    out = pl.pallas_call(
        functools.partial(_attn_kernel, n_rows=n_rows, type_starts=type_starts),
        grid=(N_HEADS_ATTN, batch, ATTN_CLASSES),
        in_specs=[pl.BlockSpec((None, 1, HEAD_DIM), lambda h, b, r: (h, 0, 0)), own,
                  cls(0), cls(1), cls(2), cls(3), cls(0), cls(1), cls(2), cls(3)],
        out_specs=own,
        out_shape=jax.ShapeDtypeStruct((batch, n_rows, ATTN_CLASSES * ATTN_WIDTH), BF16),
        scratch_shapes=[
            pltpu.VMEM((ATTN_CLASSES, padded, HEAD_DIM), BF16),
            pltpu.VMEM((ATTN_CLASSES, padded, HEAD_DIM), BF16),
            pltpu.VMEM((n_types, ATTN_BLOCK, ATTN_SAME), F32),
            pltpu.VMEM((ATTN_CLASSES, n_types, ATTN_CLASSES - 1, ATTN_BLOCK, ATTN_OTHER), F32),
        ],
        compiler_params=pltpu.CompilerParams(
            dimension_semantics=("arbitrary", "arbitrary", "arbitrary"), vmem_limit_bytes=VMEM_LIMIT_BYTES),
        name="attn",
    )(slopes, q4, k4, k4, k4, k4, v4, v4, v4, v4)
    return out.reshape(batch * seq, ATTN_WIDTH)


def _out_proj_kernel(hg_ref, at_ref, w_ref, x_ref, g_ref, b_ref, o_ref):
    mix = jnp.dot(hg_ref[...], w_ref[0:HGRN_WIDTH, :], preferred_element_type=F32)
    mix = mix + jnp.dot(at_ref[...], w_ref[HGRN_WIDTH:HGRN_WIDTH + ATTN_WIDTH, :], preferred_element_type=F32)
    o_ref[...] = _layer_norm(ALPHA * x_ref[...] + mix, g_ref[...], b_ref[...])


def _out_proj(hg, at, w_out, x, g, b, *, tm=512):
    t, d = x.shape
    tok = lambda i: (i, 0)
    const = lambda i: (0, 0)
    return pl.pallas_call(
        _out_proj_kernel,
        grid=(t // tm,),
        in_specs=[
            pl.BlockSpec((tm, HGRN_WIDTH), tok),
            pl.BlockSpec((tm, ATTN_WIDTH), tok),
            pl.BlockSpec((HGRN_WIDTH + ATTN_WIDTH, d), const),
            pl.BlockSpec((tm, d), tok),
            pl.BlockSpec((1, d), const),
            pl.BlockSpec((1, d), const),
        ],
        out_specs=pl.BlockSpec((tm, d), tok),
        out_shape=jax.ShapeDtypeStruct((t, d), F32),
        compiler_params=pltpu.CompilerParams(
            dimension_semantics=("parallel",), vmem_limit_bytes=VMEM_LIMIT_BYTES),
        name="out_proj",
    )(hg, at, w_out, x, g, b)


def _encoder_layer(x, w, layer):
    batch, seq, d = x.shape
    h = x.reshape(batch * seq, d)
    h = _ffn(h, w["ffn1_w_gate"][layer], w["ffn1_w_up"][layer], w["ffn1_w_down"][layer],
             w["ln1_g"][layer:layer + 1], w["ln1_b"][layer:layer + 1])
    hg, qa, ka, va = _in_proj(h, w["w_in"][layer])
    hgrn_out = _hgrn(hg, w["hgrn_lb_fwd"], w["hgrn_lb_bwd"], w["hgrn_norm_g"][layer], batch, seq)
    attn_out = _attn(qa, ka, va, batch, seq)
    h = _out_proj(hgrn_out.reshape(batch * seq, HGRN_WIDTH), attn_out, w["w_out"][layer], h,
                  w["ln2_g"][layer:layer + 1], w["ln2_b"][layer:layer + 1])
    h = _ffn(h, w["ffn2_w_gate"][layer], w["ffn2_w_up"][layer], w["ffn2_w_down"][layer],
             w["ln3_g"][layer:layer + 1], w["ln3_b"][layer:layer + 1])
    return h.reshape(batch, seq, d)


_MATMUL_WEIGHTS = ("ffn1_w_gate", "ffn1_w_up", "ffn1_w_down", "w_in", "w_out",
                   "ffn2_w_gate", "ffn2_w_up", "ffn2_w_down")


def kernel(x_prompt, x_sample, ln1_g, ln1_b, ffn1_w_gate, ffn1_w_up, ffn1_w_down, ln2_g, ln2_b, w_in,
           hgrn_lb_fwd, hgrn_lb_bwd, hgrn_norm_g, w_out, ln3_g, ln3_b, ffn2_w_gate, ffn2_w_up, ffn2_w_down):
    w = dict(ln1_g=ln1_g, ln1_b=ln1_b, ffn1_w_gate=ffn1_w_gate, ffn1_w_up=ffn1_w_up, ffn1_w_down=ffn1_w_down,
             ln2_g=ln2_g, ln2_b=ln2_b, w_in=w_in, hgrn_lb_fwd=hgrn_lb_fwd, hgrn_lb_bwd=hgrn_lb_bwd,
             hgrn_norm_g=hgrn_norm_g, w_out=w_out, ln3_g=ln3_g, ln3_b=ln3_b,
             ffn2_w_gate=ffn2_w_gate, ffn2_w_up=ffn2_w_up, ffn2_w_down=ffn2_w_down)
    for name in _MATMUL_WEIGHTS:
        w[name] = w[name].astype(BF16)
    outs = []
    for x in (x_prompt, x_sample):
        for layer in range(DEPTH):
            x = _encoder_layer(x, w, layer)
        outs.append(x)
    return tuple(outs)
```

```python
import functools

import numpy as np
import jax
import jax.numpy as jnp
from jax import lax
from jax.experimental import pallas as pl
from jax.experimental.pallas import tpu as pltpu

F32 = jnp.float32
BF16 = jnp.bfloat16

D_MODEL = 2048
D_FF = 5632
DEPTH = 1
HEAD_DIM = 128
N_HEADS_HGRN = 8
N_HEADS_ATTN = 8
HGRN_WIDTH = N_HEADS_HGRN * HEAD_DIM
ATTN_WIDTH = N_HEADS_ATTN * HEAD_DIM
IN_WIDTH = 5 * HGRN_WIDTH + 3 * ATTN_WIDTH
ALPHA = (2 * DEPTH) ** 0.25
LN_EPS = 1e-5
RMS_EPS = 1e-6
NEG_INF = -1e30

CHUNK = 64
HGRN_LEVELS = (1, 2, 4, 8, 16, 32)
N_LEVELS = len(HGRN_LEVELS)

ATTN_HALF = 64
ATTN_DILATIONS = (1, 4, 16)
ATTN_CLASSES = 4
ATTN_BLOCK = 128
ATTN_PAD = 256
ATTN_SAME = ATTN_BLOCK + 2 * ATTN_PAD
ATTN_OTHER_LEAD = 32
ATTN_OTHER = ATTN_BLOCK + 2 * ATTN_OTHER_LEAD
ATTN_UNROLL = 2
LANES = 128
SUBLANES = 8

VMEM_LIMIT_BYTES = 56 * 1024 * 1024


def _layer_norm(y, g, b):
    mu = jnp.mean(y, axis=-1, keepdims=True)
    yc = y - mu
    var = jnp.mean(yc * yc, axis=-1, keepdims=True)
    return yc * lax.rsqrt(var + LN_EPS) * g + b


def _dot_nt(a, b):
    return lax.dot_general(a, b, (((1,), (1,)), ((), ())), preferred_element_type=F32)


def _dot_tn(a, b):
    return lax.dot_general(a, b, (((0,), (0,)), ((), ())), preferred_element_type=F32)


def _ffn_kernel(x_ref, wg_ref, wu_ref, wd_ref, g_ref, b_ref, o_ref, xb_ref, acc_ref, *, n_ff_tiles):
    j = pl.program_id(1)

    @pl.when(j == 0)
    def _():
        xb_ref[...] = x_ref[...].astype(BF16)
        acc_ref[...] = jnp.zeros_like(acc_ref)

    xb = xb_ref[...]
    gate = jnp.dot(xb, wg_ref[...], preferred_element_type=F32)
    up = jnp.dot(xb, wu_ref[...], preferred_element_type=F32)
    hidden = (gate * jax.nn.sigmoid(gate) * up).astype(BF16)
    acc_ref[...] += jnp.dot(hidden, wd_ref[...], preferred_element_type=F32)

    @pl.when(j == n_ff_tiles - 1)
    def _():
        y = ALPHA * x_ref[...] + 0.5 * acc_ref[...]
        o_ref[...] = _layer_norm(y, g_ref[...], b_ref[...])


def _ffn(x, wg, wu, wd, g, b, *, tm=512, tf=512):
    t, d = x.shape
    n_ff_tiles = D_FF // tf
    return pl.pallas_call(
        functools.partial(_ffn_kernel, n_ff_tiles=n_ff_tiles),
        grid=(t // tm, n_ff_tiles),
        in_specs=[
            pl.BlockSpec((tm, d), lambda i, j: (i, 0)),
            pl.BlockSpec((d, tf), lambda i, j: (0, j)),
            pl.BlockSpec((d, tf), lambda i, j: (0, j)),
            pl.BlockSpec((tf, d), lambda i, j: (j, 0)),
            pl.BlockSpec((1, d), lambda i, j: (0, 0)),
            pl.BlockSpec((1, d), lambda i, j: (0, 0)),
        ],
        out_specs=pl.BlockSpec((tm, d), lambda i, j: (i, 0)),
        out_shape=jax.ShapeDtypeStruct((t, d), F32),
        scratch_shapes=[pltpu.VMEM((tm, d), BF16), pltpu.VMEM((tm, d), F32)],
        compiler_params=pltpu.CompilerParams(
            dimension_semantics=("parallel", "arbitrary"), vmem_limit_bytes=VMEM_LIMIT_BYTES),
        name="ffn",
    )(x, wg, wu, wd, g, b)


N_HGRN_PARTS = 5
IN_TILE = 1024


def _in_proj_kernel(x_ref, w_ref, hg_ref, qa_ref, ka_ref, va_ref, xb_ref):
    j = pl.program_id(1)

    @pl.when(j == 0)
    def _():
        xb_ref[...] = x_ref[...].astype(BF16)

    r = jnp.dot(xb_ref[...], w_ref[...], preferred_element_type=F32)

    @pl.when(j < N_HGRN_PARTS)
    def _():
        for c in range(N_HEADS_HGRN):
            hg_ref[c] = r[:, c * HEAD_DIM:(c + 1) * HEAD_DIM]

    for part, ref in enumerate((qa_ref, ka_ref, va_ref)):
        @pl.when(j == N_HGRN_PARTS + part)
        def _(ref=ref):
            ref[...] = r.astype(BF16)


def _in_proj(x, w_in, *, tm=512):
    t, d = x.shape
    n_col = IN_WIDTH // IN_TILE
    last_hg = N_HGRN_PARTS - 1
    tok = lambda i, j: (i, 0)
    return pl.pallas_call(
        _in_proj_kernel,
        grid=(t // tm, n_col),
        in_specs=[
            pl.BlockSpec((tm, d), tok),
            pl.BlockSpec((d, IN_TILE), lambda i, j: (0, j)),
        ],
        out_specs=[
            pl.BlockSpec((N_HEADS_HGRN, tm, HEAD_DIM), lambda i, j: (jnp.minimum(j, last_hg), i, 0)),
            pl.BlockSpec((tm, ATTN_WIDTH), tok),
            pl.BlockSpec((tm, ATTN_WIDTH), tok),
            pl.BlockSpec((tm, ATTN_WIDTH), tok),
        ],
        out_shape=[
            jax.ShapeDtypeStruct((N_HGRN_PARTS * N_HEADS_HGRN, t, HEAD_DIM), F32),
            jax.ShapeDtypeStruct((t, ATTN_WIDTH), BF16),
            jax.ShapeDtypeStruct((t, ATTN_WIDTH), BF16),
            jax.ShapeDtypeStruct((t, ATTN_WIDTH), BF16),
        ],
        scratch_shapes=[pltpu.VMEM((tm, d), BF16)],
        compiler_params=pltpu.CompilerParams(
            dimension_semantics=("parallel", "arbitrary"), vmem_limit_bytes=VMEM_LIMIT_BYTES),
        name="in_proj",
    )(x, w_in)


def _hgrn_masks():
    t = np.arange(CHUNK)[:, None]
    u = np.arange(CHUNK)[None, :]
    masks = [(t == u)]
    for c in HGRN_LEVELS:
        same = (t // (2 * c)) == (u // (2 * c))
        masks.append(same & ((t % (2 * c)) >= c) & ((u % (2 * c)) < c))
    return np.stack(masks, axis=0).astype(np.float32)


def _rows(x, r, n):
    return jnp.broadcast_to(x[r:r + 1], (n, x.shape[-1]))


def _chunk_cumsum(x, reverse):
    sub = lax.broadcasted_iota(jnp.int32, x.shape, 0) % SUBLANES
    step = 1
    while step < SUBLANES:
        if reverse:
            x = x + jnp.where(sub < SUBLANES - step, pltpu.roll(x, CHUNK - step, axis=0), 0.0)
        else:
            x = x + jnp.where(sub >= step, pltpu.roll(x, step, axis=0), 0.0)
        step *= 2
    tiles = [x[j:j + SUBLANES] for j in range(0, CHUNK, SUBLANES)]
    if reverse:
        tiles = tiles[::-1]
    edge = 0 if reverse else SUBLANES - 1
    for j in range(1, len(tiles)):
        tiles[j] = tiles[j] + _rows(tiles[j - 1], edge, SUBLANES)
    if reverse:
        tiles = tiles[::-1]
    return jnp.concatenate(tiles, axis=0)


def _query_side(c, reverse):
    t = np.arange(CHUNK)
    return ((t % (2 * c)) >= c) != reverse


def _hgrn_signs(reverse):
    log2e = np.float32(np.log2(np.e))
    sign = np.stack([np.where(_query_side(c, reverse), log2e, -log2e) for c in HGRN_LEVELS])
    return np.broadcast_to(sign[:, :, None], (N_LEVELS, CHUNK, HEAD_DIM)).astype(np.float32)


def _level_boundary(g, c, reverse):
    first = c if reverse else c - 1
    if 2 * c >= SUBLANES:
        return jnp.concatenate([_rows(g, b, 2 * c) for b in range(first, CHUNK, 2 * c)], axis=0)
    sub = lax.broadcasted_iota(jnp.int32, (SUBLANES, g.shape[-1]), 0)
    return jnp.concatenate(
        [jnp.where(sub < 2 * c, _rows(g, j + first, SUBLANES), _rows(g, j + 2 * c + first, SUBLANES))
         for j in range(0, CHUNK, SUBLANES)], axis=0)


def _hgrn_chunk(q, z, v, lb, masks, signs, state_ref, reverse):
    f = lb + (1.0 - lb) * jax.nn.sigmoid(z)
    k = 1.0 - f
    g_incl = _chunk_cumsum(jnp.log(f), reverse)
    last_row = 0 if reverse else CHUNK - 1
    decay_in = jnp.exp(g_incl)
    decay_out = jnp.exp(_rows(g_incl, last_row, CHUNK) - g_incl)
    decay_all = jnp.exp(g_incl[last_row:last_row + 1])

    qb, kb, vb = q.astype(BF16), k.astype(BF16), v.astype(BF16)
    diag = _dot_nt(qb, kb) * masks[0]
    a = [diag[j:j + SUBLANES] for j in range(0, CHUNK, SUBLANES)]
    sub = lax.broadcasted_iota(jnp.int32, q.shape, 0)
    for lv, c in enumerate(HGRN_LEVELS):
        mask = masks[lv + 1]
        query_side = _query_side(c, reverse)
        if c < SUBLANES:
            rows_q = ((sub % (2 * c)) >= c) != reverse
            if c == 1:
                mixed = jnp.where(rows_q, q * f, k)
            else:
                decay = jnp.exp2((g_incl - _level_boundary(g_incl, c, reverse)) * signs[lv])
                mixed = jnp.where(rows_q, q, k) * decay
            mixed = mixed.astype(BF16)
            prod = _dot_nt(mixed, mixed) * mask
            a = [a[j] + prod[j * SUBLANES:(j + 1) * SUBLANES] for j in range(len(a))]
            continue
        decay = jnp.exp2((g_incl - _level_boundary(g_incl, c, reverse)) * signs[lv])
        mixed = jnp.concatenate([(q if query_side[r] else k)[r:r + c] for r in range(0, CHUNK, c)], axis=0) * decay
        q_rows = [r for r in range(0, CHUNK, SUBLANES) if query_side[r]]
        lhs = jnp.concatenate([mixed[r:r + SUBLANES] for r in q_rows], axis=0).astype(BF16)
        prod = _dot_nt(lhs, mixed.astype(BF16))
        for i, r in enumerate(q_rows):
            j = r // SUBLANES
            a[j] = a[j] + prod[i * SUBLANES:(i + 1) * SUBLANES] * mask[r:r + SUBLANES]
    a = jnp.concatenate(a, axis=0)
    state = state_ref[...]
    out = jnp.dot(a.astype(BF16), vb, preferred_element_type=F32)
    out = out + _dot_nt((q * decay_in).astype(BF16), state.astype(BF16))
    state_ref[...] = state * decay_all + _dot_tn(vb, (k * decay_out).astype(BF16))
    return out


def _hgrn_kernel(qf_ref, zf_ref, vf_ref, gf_ref, qb_ref, zb_ref, vb_ref, gb_ref, lbf_ref, lbb_ref, ng_ref,
                 mf_ref, mb_ref, sgf_ref, sgb_ref, o_ref, part_ref, stf_ref, stb_ref, *, n_tiles, tile):
    s = pl.program_id(2)
    n_heads = qf_ref.shape[0]
    chunks = tile // CHUNK

    @pl.when(s == 0)
    def _():
        stf_ref[...] = jnp.zeros_like(stf_ref)
        stb_ref[...] = jnp.zeros_like(stb_ref)

    def lower_bound(ref, h):
        p = ref[h]
        p = jnp.exp(p - jnp.max(p, axis=0, keepdims=True))
        return p[0:1] / jnp.sum(p, axis=0, keepdims=True)

    def scan(finish):
        def scan_step(c, carry):
            lf = pl.multiple_of(c * CHUNK, CHUNK)
            lb = pl.multiple_of((chunks - 1 - c) * CHUNK, CHUNK)
            rows_f = pl.ds(lf, CHUNK)
            rows_b = pl.ds(lb, CHUNK)
            out_f = pl.ds(pl.multiple_of(s * tile + lf, CHUNK), CHUNK)
            out_b = pl.ds(pl.multiple_of((n_tiles - 1 - s) * tile + lb, CHUNK), CHUNK)
            res = []
            for h in range(n_heads):
                res.append(_hgrn_chunk(qf_ref[h, rows_f, :], zf_ref[h, rows_f, :], vf_ref[h, rows_f, :],
                                       lower_bound(lbf_ref, h), mf_ref[...], sgf_ref[...], stf_ref.at[h], False))
                res.append(_hgrn_chunk(qb_ref[h, rows_b, :], zb_ref[h, rows_b, :], vb_ref[h, rows_b, :],
                                       lower_bound(lbb_ref, h), mb_ref[...], sgb_ref[...], stb_ref.at[h], True))
            for h in range(n_heads):
                for o, rows, out_rows, g_ref in ((res[2 * h], rows_f, out_f, gf_ref),
                                                 (res[2 * h + 1], rows_b, out_b, gb_ref)):
                    if not finish:
                        part_ref[h, out_rows, :] = o
                        continue
                    o = o + part_ref[h, out_rows, :]
                    o = o * lax.rsqrt(jnp.mean(o * o, axis=-1, keepdims=True) + RMS_EPS)
                    gate = g_ref[h, rows, :]
                    o = o * ng_ref[h] * (gate * jax.nn.sigmoid(gate))
                    o_ref[out_rows, h * HEAD_DIM:(h + 1) * HEAD_DIM] = o.astype(BF16)
            return carry

        lax.fori_loop(0, chunks, scan_step, 0, unroll=HGRN_UNROLL)

    @pl.when(s < n_tiles // 2)
    def _():
        scan(finish=False)

    @pl.when(s >= n_tiles // 2)
    def _():
        scan(finish=True)


HGRN_HEADS_PER_STEP = 4
HGRN_TILE = 512
HGRN_UNROLL = 2


def _hgrn(hg, lb_fwd, lb_bwd, norm_g, batch, seq):
    hg5 = hg.reshape(N_HGRN_PARTS, N_HEADS_HGRN, batch, seq, HEAD_DIM)
    masks = _hgrn_masks()
    masks_f = jnp.asarray(masks, F32)
    masks_b = jnp.asarray(masks[:, ::-1, ::-1], F32)
    hps, tile = HGRN_HEADS_PER_STEP, HGRN_TILE
    n_tiles = seq // tile
    assert seq % tile == 0 and n_tiles % 2 == 0 and N_HEADS_HGRN % hps == 0

    def part(p, backward):
        def index(b, hg_, s):
            return (p, hg_, b, (n_tiles - 1 - s) if backward else s, 0)
        return pl.BlockSpec((None, hps, None, tile, HEAD_DIM), index)

    def per_head(rows):
        return pl.BlockSpec((hps, rows, HEAD_DIM), lambda b, hg_, s: (hg_, 0, 0))

    def lb_per_head(lb):
        return lb.reshape(DEPTH + 1, N_HEADS_HGRN, HEAD_DIM).transpose(1, 0, 2)

    const3 = lambda b, hg_, s: (0, 0, 0)
    q, zf, zb, v, g = range(N_HGRN_PARTS)
    return pl.pallas_call(
        functools.partial(_hgrn_kernel, n_tiles=n_tiles, tile=tile),
        grid=(batch, N_HEADS_HGRN // hps, n_tiles),
        in_specs=[
            part(q, False), part(zf, False), part(v, False), part(g, False),
            part(q, True), part(zb, True), part(v, True), part(g, True),
            per_head(DEPTH + 1), per_head(DEPTH + 1), per_head(1),
            pl.BlockSpec((N_LEVELS + 1, CHUNK, CHUNK), const3),
            pl.BlockSpec((N_LEVELS + 1, CHUNK, CHUNK), const3),
            pl.BlockSpec((N_LEVELS, CHUNK, HEAD_DIM), const3),
            pl.BlockSpec((N_LEVELS, CHUNK, HEAD_DIM), const3),
        ],
        out_specs=pl.BlockSpec((None, seq, hps * HEAD_DIM), lambda b, hg_, s: (b, 0, hg_)),
        out_shape=jax.ShapeDtypeStruct((batch, seq, HGRN_WIDTH), BF16),
        scratch_shapes=[
            pltpu.VMEM((hps, seq, HEAD_DIM), F32),
            pltpu.VMEM((hps, HEAD_DIM, HEAD_DIM), F32), pltpu.VMEM((hps, HEAD_DIM, HEAD_DIM), F32),
        ],
        compiler_params=pltpu.CompilerParams(
            dimension_semantics=("parallel", "parallel", "arbitrary"), vmem_limit_bytes=VMEM_LIMIT_BYTES),
        name="hgrn",
    )(hg5, hg5, hg5, hg5, hg5, hg5, hg5, hg5,
      lb_per_head(lb_fwd), lb_per_head(lb_bwd), norm_g.reshape(N_HEADS_HGRN, 1, HEAD_DIM),
      masks_f, masks_b, jnp.asarray(_hgrn_signs(False)), jnp.asarray(_hgrn_signs(True)))


def _attn_block_types(n_rows):
    row = np.arange(ATTN_BLOCK)

    def flags(n):
        i = n * ATTN_BLOCK + row
        reach = [ATTN_HALF * d // ATTN_CLASSES for d in ATTN_DILATIONS]
        return np.stack([i < r for r in reach] + [i >= n_rows - r for r in reach]).tobytes()

    starts = [0]
    for n in range(1, n_rows // ATTN_BLOCK):
        if flags(n) != flags(n - 1):
            starts.append(n)
    return tuple(starts)


def _attn_bias(delta, i_query, slope, n_rows):
    mult = jnp.zeros(delta.shape, F32)
    for d in ATTN_DILATIONS:
        reach = ATTN_HALF * d // ATTN_CLASSES
        lo = jnp.where(i_query < reach, 0, -ATTN_HALF * d)
        hi = jnp.where(i_query >= n_rows - reach, (ATTN_HALF - 1) * d, ATTN_HALF * d)
        hit = (delta >= lo) & (delta <= hi)
        if d > 1:
            hit = hit & (lax.rem(delta, d) == 0)
        mult = mult + hit.astype(F32)
    dist = jnp.abs(delta).astype(F32)
    return jnp.where(mult > 0.0, jnp.log(jnp.maximum(mult, 1.0)) - slope * dist, NEG_INF)


def _lane_reduce(parts, combine, reduce):
    folded = {}
    for part in parts:
        for j in range(0, part.shape[-1], LANES):
            tile = part[:, j:j + LANES]
            w = tile.shape[-1]
            folded[w] = tile if w not in folded else combine(folded[w], tile)
    out = None
    for tile in folded.values():
        r = reduce(tile, axis=-1, keepdims=True)
        out = r if out is None else combine(out, r)
    return out


def _attn_kernel(slope_ref, q_ref, k0_ref, k1_ref, k2_ref, k3_ref, v0_ref, v1_ref, v2_ref, v3_ref,
                 o_ref, kpad_ref, vpad_ref, bias_same_ref, bias_other_ref, *, n_rows, type_starts):
    b, rq = pl.program_id(1), pl.program_id(2)

    @pl.when((b == 0) & (rq == 0))
    def _():
        slope = slope_ref[...][:, 0:1]
        row = lax.broadcasted_iota(jnp.int32, (ATTN_BLOCK, ATTN_SAME), 0)
        col = lax.broadcasted_iota(jnp.int32, (ATTN_BLOCK, ATTN_SAME), 1)
        for ty, first_block in enumerate(type_starts):
            bias_same_ref[ty] = _attn_bias((col - ATTN_PAD - row) * ATTN_CLASSES,
                                           first_block * ATTN_BLOCK + row, slope, n_rows)
        row = lax.broadcasted_iota(jnp.int32, (ATTN_BLOCK, ATTN_OTHER), 0)
        col = lax.broadcasted_iota(jnp.int32, (ATTN_BLOCK, ATTN_OTHER), 1)
        for query_class in range(ATTN_CLASSES):
            for m in range(1, ATTN_CLASSES):
                delta = (col - ATTN_OTHER_LEAD - row) * ATTN_CLASSES + ((query_class + m) % ATTN_CLASSES - query_class)
                for ty, first_block in enumerate(type_starts):
                    bias_other_ref[query_class, ty, m - 1] = _attn_bias(
                        delta, first_block * ATTN_BLOCK + row, slope, n_rows)

    @pl.when(rq == 0)
    def _():
        zeros = jnp.zeros((ATTN_PAD, HEAD_DIM), BF16)
        for c, (k_ref, v_ref) in enumerate(((k0_ref, v0_ref), (k1_ref, v1_ref), (k2_ref, v2_ref), (k3_ref, v3_ref))):
            for pad_ref, src_ref in ((kpad_ref, k_ref), (vpad_ref, v_ref)):
                pad_ref[c, 0:ATTN_PAD, :] = zeros
                pad_ref[c, ATTN_PAD:ATTN_PAD + n_rows, :] = src_ref[...]
                pad_ref[c, ATTN_PAD + n_rows:ATTN_PAD + n_rows + ATTN_PAD, :] = zeros

    scale = HEAD_DIM ** -0.5

    def block(n, carry):
        i0 = pl.multiple_of(n * ATTN_BLOCK, ATTN_BLOCK)
        ty = jnp.int32(0)
        for first_block in type_starts[1:]:
            ty = ty + jnp.asarray(n >= first_block, jnp.int32)
        qb = q_ref[pl.ds(i0, ATTN_BLOCK), :]
        scores, values = [], []
        s = _dot_nt(qb, kpad_ref[rq, pl.ds(i0, ATTN_SAME), :]) * scale
        scores.append(s + bias_same_ref[ty])
        values.append(vpad_ref[rq, pl.ds(i0, ATTN_SAME), :])
        start = pl.multiple_of(i0 + (ATTN_PAD - ATTN_OTHER_LEAD), ATTN_OTHER_LEAD)
        for m in range(1, ATTN_CLASSES):
            rk = lax.rem(rq + m, ATTN_CLASSES)
            s = _dot_nt(qb, kpad_ref[rk, pl.ds(start, ATTN_OTHER), :]) * scale
            scores.append(s + bias_other_ref[rq, ty, m - 1])
            values.append(vpad_ref[rk, pl.ds(start, ATTN_OTHER), :])
        mx = _lane_reduce(scores, jnp.maximum, jnp.max)
        probs = [jnp.exp(s - mx) for s in scores]
        den = _lane_reduce(probs, jnp.add, jnp.sum)
        acc = jnp.zeros((ATTN_BLOCK, HEAD_DIM), F32)
        for p, v in zip(probs, values):
            acc = acc + jnp.dot(p.astype(BF16), v, preferred_element_type=F32)
        o_ref[pl.ds(i0, ATTN_BLOCK), :] = (acc / den).astype(BF16)
        return carry

    lax.fori_loop(0, n_rows // ATTN_BLOCK, block, 0, unroll=ATTN_UNROLL)


def _attn(qa, ka, va, batch, seq):
    n_rows = seq // ATTN_CLASSES
    assert seq % (ATTN_HALF * max(ATTN_DILATIONS)) == 0 and seq >= 2 * ATTN_HALF * max(ATTN_DILATIONS)
    assert n_rows % (ATTN_BLOCK * ATTN_UNROLL) == 0
    type_starts = _attn_block_types(n_rows)
    n_types = len(type_starts)
    view = lambda a: a.reshape(batch, n_rows, ATTN_CLASSES * ATTN_WIDTH)
    slopes = 2.0 ** (-8.0 * (jnp.arange(N_HEADS_ATTN, dtype=F32) + 1.0) / N_HEADS_ATTN)
    slopes = jnp.broadcast_to(slopes[:, None, None], (N_HEADS_ATTN, 1, HEAD_DIM))

    def cls(c):
        return pl.BlockSpec((None, n_rows, HEAD_DIM), lambda h, b, r: (b, 0, c * N_HEADS_ATTN + h))

    own = pl.BlockSpec((None, n_rows, HEAD_DIM), lambda h, b, r: (b, 0, r * N_HEADS_ATTN + h))
    padded = n_rows + 2 * ATTN_PAD
    q4, k4, v4 = view(qa), view(ka), view(va)
    out = pl.pallas_call(
        functools.partial(_attn_kernel, n_rows=n_rows, type_starts=type_starts),
        grid=(N_HEADS_ATTN, batch, ATTN_CLASSES),
        in_specs=[pl.BlockSpec((None, 1, HEAD_DIM), lambda h, b, r: (h, 0, 0)), own,
                  cls(0), cls(1), cls(2), cls(3), cls(0), cls(1), cls(2), cls(3)],
        out_specs=own,
        out_shape=jax.ShapeDtypeStruct((batch, n_rows, ATTN_CLASSES * ATTN_WIDTH), BF16),
        scratch_shapes=[
            pltpu.VMEM((ATTN_CLASSES, padded, HEAD_DIM), BF16),
            pltpu.VMEM((ATTN_CLASSES, padded, HEAD_DIM), BF16),
            pltpu.VMEM((n_types, ATTN_BLOCK, ATTN_SAME), F32),
            pltpu.VMEM((ATTN_CLASSES, n_types, ATTN_CLASSES - 1, ATTN_BLOCK, ATTN_OTHER), F32),
        ],
        compiler_params=pltpu.CompilerParams(
            dimension_semantics=("arbitrary", "arbitrary", "arbitrary"), vmem_limit_bytes=VMEM_LIMIT_BYTES),
        name="attn",
    )(slopes, q4, k4, k4, k4, k4, v4, v4, v4, v4)
    return out.reshape(batch * seq, ATTN_WIDTH)


def _out_proj_kernel(hg_ref, at_ref, w_ref, x_ref, g_ref, b_ref, o_ref):
    mix = jnp.dot(hg_ref[...], w_ref[0:HGRN_WIDTH, :], preferred_element_type=F32)
    mix = mix + jnp.dot(at_ref[...], w_ref[HGRN_WIDTH:HGRN_WIDTH + ATTN_WIDTH, :], preferred_element_type=F32)
    o_ref[...] = _layer_norm(ALPHA * x_ref[...] + mix, g_ref[...], b_ref[...])


def _out_proj(hg, at, w_out, x, g, b, *, tm=512):
    t, d = x.shape
    tok = lambda i: (i, 0)
    const = lambda i: (0, 0)
    return pl.pallas_call(
        _out_proj_kernel,
        grid=(t // tm,),
        in_specs=[
            pl.BlockSpec((tm, HGRN_WIDTH), tok),
            pl.BlockSpec((tm, ATTN_WIDTH), tok),
            pl.BlockSpec((HGRN_WIDTH + ATTN_WIDTH, d), const),
            pl.BlockSpec((tm, d), tok),
            pl.BlockSpec((1, d), const),
            pl.BlockSpec((1, d), const),
        ],
        out_specs=pl.BlockSpec((tm, d), tok),
        out_shape=jax.ShapeDtypeStruct((t, d), F32),
        compiler_params=pltpu.CompilerParams(
            dimension_semantics=("parallel",), vmem_limit_bytes=VMEM_LIMIT_BYTES),
        name="out_proj",
    )(hg, at, w_out, x, g, b)


def _encoder_layer(x, w, layer):
    batch, seq, d = x.shape
    h = x.reshape(batch * seq, d)
    h = _ffn(h, w["ffn1_w_gate"][layer], w["ffn1_w_up"][layer], w["ffn1_w_down"][layer],
             w["ln1_g"][layer:layer + 1], w["ln1_b"][layer:layer + 1])
    hg, qa, ka, va = _in_proj(h, w["w_in"][layer])
    hgrn_out = _hgrn(hg, w["hgrn_lb_fwd"], w["hgrn_lb_bwd"], w["hgrn_norm_g"][layer], batch, seq)
    attn_out = _attn(qa, ka, va, batch, seq)
    h = _out_proj(hgrn_out.reshape(batch * seq, HGRN_WIDTH), attn_out, w["w_out"][layer], h,
                  w["ln2_g"][layer:layer + 1], w["ln2_b"][layer:layer + 1])
    h = _ffn(h, w["ffn2_w_gate"][layer], w["ffn2_w_up"][layer], w["ffn2_w_down"][layer],
             w["ln3_g"][layer:layer + 1], w["ln3_b"][layer:layer + 1])
    return h.reshape(batch, seq, d)


_MATMUL_WEIGHTS = ("ffn1_w_gate", "ffn1_w_up", "ffn1_w_down", "w_in", "w_out",
                   "ffn2_w_gate", "ffn2_w_up", "ffn2_w_down")


def kernel(x_prompt, x_sample, ln1_g, ln1_b, ffn1_w_gate, ffn1_w_up, ffn1_w_down, ln2_g, ln2_b, w_in,
           hgrn_lb_fwd, hgrn_lb_bwd, hgrn_norm_g, w_out, ln3_g, ln3_b, ffn2_w_gate, ffn2_w_up, ffn2_w_down):
    w = dict(ln1_g=ln1_g, ln1_b=ln1_b, ffn1_w_gate=ffn1_w_gate, ffn1_w_up=ffn1_w_up, ffn1_w_down=ffn1_w_down,
             ln2_g=ln2_g, ln2_b=ln2_b, w_in=w_in, hgrn_lb_fwd=hgrn_lb_fwd, hgrn_lb_bwd=hgrn_lb_bwd,
             hgrn_norm_g=hgrn_norm_g, w_out=w_out, ln3_g=ln3_g, ln3_b=ln3_b,
             ffn2_w_gate=ffn2_w_gate, ffn2_w_up=ffn2_w_up, ffn2_w_down=ffn2_w_down)
    for name in _MATMUL_WEIGHTS:
        w[name] = w[name].astype(BF16)
    outs = []
    for x in (x_prompt, x_sample):
        for layer in range(DEPTH):
            x = _encoder_layer(x, w, layer)
        outs.append(x)
    return tuple(outs)
```

```python
import functools

import numpy as np
import jax
import jax.numpy as jnp
from jax import lax
from jax.experimental import pallas as pl
from jax.experimental.pallas import tpu as pltpu

F32 = jnp.float32
BF16 = jnp.bfloat16

D_MODEL = 2048
D_FF = 5632
DEPTH = 1
HEAD_DIM = 128
N_HEADS_HGRN = 8
N_HEADS_ATTN = 8
HGRN_WIDTH = N_HEADS_HGRN * HEAD_DIM
ATTN_WIDTH = N_HEADS_ATTN * HEAD_DIM
IN_WIDTH = 5 * HGRN_WIDTH + 3 * ATTN_WIDTH
ALPHA = (2 * DEPTH) ** 0.25
LN_EPS = 1e-5
RMS_EPS = 1e-6
NEG_INF = -1e30

CHUNK = 64
HGRN_LEVELS = (1, 2, 4, 8, 16, 32)
N_LEVELS = len(HGRN_LEVELS)

ATTN_HALF = 64
ATTN_DILATIONS = (1, 4, 16)
ATTN_CLASSES = 4
ATTN_BLOCK = 128
ATTN_PAD = 256
ATTN_SAME = ATTN_BLOCK + 2 * ATTN_PAD
ATTN_OTHER_LEAD = 32
ATTN_OTHER = ATTN_BLOCK + 2 * ATTN_OTHER_LEAD
ATTN_UNROLL = 2
LANES = 128
SUBLANES = 8

VMEM_LIMIT_BYTES = 56 * 1024 * 1024


def _layer_norm(y, g, b):
    mu = jnp.mean(y, axis=-1, keepdims=True)
    yc = y - mu
    var = jnp.mean(yc * yc, axis=-1, keepdims=True)
    return yc * lax.rsqrt(var + LN_EPS) * g + b


def _dot_nt(a, b):
    return lax.dot_general(a, b, (((1,), (1,)), ((), ())), preferred_element_type=F32)


def _dot_tn(a, b):
    return lax.dot_general(a, b, (((0,), (0,)), ((), ())), preferred_element_type=F32)


def _ffn_kernel(x_ref, wg_ref, wu_ref, wd_ref, g_ref, b_ref, *refs, n_ff_tiles, emit_bf16):
    if emit_bf16:
        o_ref, ob_ref, xb_ref, acc_ref = refs
    else:
        o_ref, xb_ref, acc_ref = refs
    j = pl.program_id(1)

    @pl.when(j == 0)
    def _():
        xb_ref[...] = x_ref[...].astype(BF16)
        acc_ref[...] = jnp.zeros_like(acc_ref)

    xb = xb_ref[...]
    gate = jnp.dot(xb, wg_ref[...], preferred_element_type=F32)
    up = jnp.dot(xb, wu_ref[...], preferred_element_type=F32)
    hidden = (gate * jax.nn.sigmoid(gate) * up).astype(BF16)
    acc_ref[...] += jnp.dot(hidden, wd_ref[...], preferred_element_type=F32)

    @pl.when(j == n_ff_tiles - 1)
    def _():
        y = _layer_norm(ALPHA * x_ref[...] + 0.5 * acc_ref[...], g_ref[...], b_ref[...])
        o_ref[...] = y
        if emit_bf16:
            ob_ref[...] = y.astype(BF16)


def _ffn(x, wg, wu, wd, g, b, *, emit_bf16, tm=512, tf=512):
    t, d = x.shape
    n_ff_tiles = D_FF // tf
    tok = lambda i, j: (i, 0)
    out_specs = [pl.BlockSpec((tm, d), tok)]
    out_shape = [jax.ShapeDtypeStruct((t, d), F32)]
    if emit_bf16:
        out_specs.append(pl.BlockSpec((tm, d), tok))
        out_shape.append(jax.ShapeDtypeStruct((t, d), BF16))
    return pl.pallas_call(
        functools.partial(_ffn_kernel, n_ff_tiles=n_ff_tiles, emit_bf16=emit_bf16),
        grid=(t // tm, n_ff_tiles),
        in_specs=[
            pl.BlockSpec((tm, d), tok),
            pl.BlockSpec((d, tf), lambda i, j: (0, j)),
            pl.BlockSpec((d, tf), lambda i, j: (0, j)),
            pl.BlockSpec((tf, d), lambda i, j: (j, 0)),
            pl.BlockSpec((1, d), lambda i, j: (0, 0)),
            pl.BlockSpec((1, d), lambda i, j: (0, 0)),
        ],
        out_specs=out_specs,
        out_shape=out_shape,
        scratch_shapes=[pltpu.VMEM((tm, d), BF16), pltpu.VMEM((tm, d), F32)],
        compiler_params=pltpu.CompilerParams(
            dimension_semantics=("parallel", "arbitrary"), vmem_limit_bytes=VMEM_LIMIT_BYTES),
        name="ffn",
    )(x, wg, wu, wd, g, b)


N_HGRN_PARTS = 5
N_ATTN_PARTS = 3
IN_TILE = 1024


def _proj_heads_kernel(x_ref, w_ref, o_ref):
    r = jnp.dot(x_ref[...], w_ref[...], preferred_element_type=F32)
    for c in range(N_HEADS_HGRN):
        o_ref[c] = r[:, c * HEAD_DIM:(c + 1) * HEAD_DIM]


def _store_lane_tiles(ref, x):
    for j in range(ref.shape[0]):
        ref[j] = x[:, j * LANES:(j + 1) * LANES]


def _proj_classes_kernel(x_ref, w_ref, o_ref, r_ref):
    _store_lane_tiles(r_ref, jnp.dot(x_ref[...], w_ref[...], preferred_element_type=F32))
    rows = o_ref.shape[0]
    for c in range(ATTN_CLASSES):
        cls_rows = pl.ds(c, rows, stride=ATTN_CLASSES)
        for j in range(r_ref.shape[0]):
            lo = c * ATTN_WIDTH + j * LANES
            o_ref[:, lo:lo + LANES] = r_ref[j, cls_rows, :].astype(BF16)


def _in_proj(xb, w_in, *, tm=512):
    t, d = xb.shape
    params = pltpu.CompilerParams(dimension_semantics=("parallel", "arbitrary"), vmem_limit_bytes=VMEM_LIMIT_BYTES)
    hg = pl.pallas_call(
        _proj_heads_kernel,
        grid=(t // tm, N_HGRN_PARTS),
        in_specs=[pl.BlockSpec((tm, d), lambda i, j: (i, 0)), pl.BlockSpec((d, IN_TILE), lambda i, j: (0, j))],
        out_specs=pl.BlockSpec((N_HEADS_HGRN, tm, HEAD_DIM), lambda i, j: (j, i, 0)),
        out_shape=jax.ShapeDtypeStruct((N_HGRN_PARTS * N_HEADS_HGRN, t, HEAD_DIM), F32),
        compiler_params=params,
        name="in_proj_hgrn",
    )(xb, w_in)
    qkv = pl.pallas_call(
        _proj_classes_kernel,
        grid=(t // tm, N_ATTN_PARTS),
        in_specs=[pl.BlockSpec((tm, d), lambda i, j: (i, 0)),
                  pl.BlockSpec((d, IN_TILE), lambda i, j: (0, N_HGRN_PARTS + j))],
        out_specs=pl.BlockSpec((None, tm // ATTN_CLASSES, ATTN_CLASSES * ATTN_WIDTH), lambda i, j: (j, i, 0)),
        out_shape=jax.ShapeDtypeStruct((N_ATTN_PARTS, t // ATTN_CLASSES, ATTN_CLASSES * ATTN_WIDTH), BF16),
        scratch_shapes=[pltpu.VMEM((IN_TILE // LANES, tm, LANES), F32)],
        compiler_params=params,
        name="in_proj_attn",
    )(xb, w_in)
    return hg, qkv


def _hgrn_masks():
    t = np.arange(CHUNK)[:, None]
    u = np.arange(CHUNK)[None, :]
    masks = [(t == u)]
    for c in HGRN_LEVELS:
        same = (t // (2 * c)) == (u // (2 * c))
        masks.append(same & ((t % (2 * c)) >= c) & ((u % (2 * c)) < c))
    return np.stack(masks, axis=0).astype(np.float32)


def _rows(x, r, n):
    return jnp.broadcast_to(x[r:r + 1], (n, x.shape[-1]))


def _chunk_cumsum(x, reverse):
    sub = lax.broadcasted_iota(jnp.int32, x.shape, 0) % SUBLANES
    step = 1
    while step < SUBLANES:
        if reverse:
            x = x + jnp.where(sub < SUBLANES - step, pltpu.roll(x, CHUNK - step, axis=0), 0.0)
        else:
            x = x + jnp.where(sub >= step, pltpu.roll(x, step, axis=0), 0.0)
        step *= 2
    tiles = [x[j:j + SUBLANES] for j in range(0, CHUNK, SUBLANES)]
    if reverse:
        tiles = tiles[::-1]
    edge = 0 if reverse else SUBLANES - 1
    for j in range(1, len(tiles)):
        tiles[j] = tiles[j] + _rows(tiles[j - 1], edge, SUBLANES)
    if reverse:
        tiles = tiles[::-1]
    return jnp.concatenate(tiles, axis=0)


def _query_side(c, reverse):
    t = np.arange(CHUNK)
    return ((t % (2 * c)) >= c) != reverse


def _hgrn_signs(reverse):
    log2e = np.float32(np.log2(np.e))
    sign = np.stack([np.where(_query_side(c, reverse), log2e, -log2e) for c in HGRN_LEVELS])
    return np.broadcast_to(sign[:, :, None], (N_LEVELS, CHUNK, HEAD_DIM)).astype(np.float32)


def _level_boundary(g, c, reverse):
    first = c if reverse else c - 1
    if 2 * c >= SUBLANES:
        return jnp.concatenate([_rows(g, b, 2 * c) for b in range(first, CHUNK, 2 * c)], axis=0)
    sub = lax.broadcasted_iota(jnp.int32, (SUBLANES, g.shape[-1]), 0)
    return jnp.concatenate(
        [jnp.where(sub < 2 * c, _rows(g, j + first, SUBLANES), _rows(g, j + 2 * c + first, SUBLANES))
         for j in range(0, CHUNK, SUBLANES)], axis=0)


def _hgrn_chunk(q, z, v, lb, masks, signs, state_ref, reverse):
    f = lb + (1.0 - lb) * jax.nn.sigmoid(z)
    k = 1.0 - f
    g_incl = _chunk_cumsum(jnp.log(f), reverse)
    last_row = 0 if reverse else CHUNK - 1
    decay_in = jnp.exp(g_incl)
    decay_out = jnp.exp(_rows(g_incl, last_row, CHUNK) - g_incl)
    decay_all = jnp.exp(g_incl[last_row:last_row + 1])

    qb, kb, vb = q.astype(BF16), k.astype(BF16), v.astype(BF16)
    diag = _dot_nt(qb, kb) * masks[0]
    a = [diag[j:j + SUBLANES] for j in range(0, CHUNK, SUBLANES)]
    sub = lax.broadcasted_iota(jnp.int32, q.shape, 0)
    for lv, c in enumerate(HGRN_LEVELS):
        mask = masks[lv + 1]
        query_side = _query_side(c, reverse)
        if c < SUBLANES:
            rows_q = ((sub % (2 * c)) >= c) != reverse
            if c == 1:
                mixed = jnp.where(rows_q, q * f, k)
            else:
                decay = jnp.exp2((g_incl - _level_boundary(g_incl, c, reverse)) * signs[lv])
                mixed = jnp.where(rows_q, q, k) * decay
            mixed = mixed.astype(BF16)
            prod = _dot_nt(mixed, mixed) * mask
            a = [a[j] + prod[j * SUBLANES:(j + 1) * SUBLANES] for j in range(len(a))]
            continue
        decay = jnp.exp2((g_incl - _level_boundary(g_incl, c, reverse)) * signs[lv])
        mixed = jnp.concatenate([(q if query_side[r] else k)[r:r + c] for r in range(0, CHUNK, c)], axis=0) * decay
        q_rows = [r for r in range(0, CHUNK, SUBLANES) if query_side[r]]
        lhs = jnp.concatenate([mixed[r:r + SUBLANES] for r in q_rows], axis=0).astype(BF16)
        prod = _dot_nt(lhs, mixed.astype(BF16))
        for i, r in enumerate(q_rows):
            j = r // SUBLANES
            a[j] = a[j] + prod[i * SUBLANES:(i + 1) * SUBLANES] * mask[r:r + SUBLANES]
    a = jnp.concatenate(a, axis=0)
    state = state_ref[...]
    out = jnp.dot(a.astype(BF16), vb, preferred_element_type=F32)
    out = out + _dot_nt((q * decay_in).astype(BF16), state.astype(BF16))
    state_ref[...] = state * decay_all + _dot_tn(vb, (k * decay_out).astype(BF16))
    return out


def _hgrn_kernel(qf_ref, zf_ref, vf_ref, gf_ref, qb_ref, zb_ref, vb_ref, gb_ref, lbf_ref, lbb_ref, ng_ref,
                 mf_ref, mb_ref, sgf_ref, sgb_ref, o_ref, part_ref, stf_ref, stb_ref, *, n_tiles, tile):
    s = pl.program_id(2)
    n_heads = qf_ref.shape[0]
    chunks = tile // CHUNK

    @pl.when(s == 0)
    def _():
        stf_ref[...] = jnp.zeros_like(stf_ref)
        stb_ref[...] = jnp.zeros_like(stb_ref)

    def lower_bound(ref, h):
        p = ref[h]
        p = jnp.exp(p - jnp.max(p, axis=0, keepdims=True))
        return p[0:1] / jnp.sum(p, axis=0, keepdims=True)

    def scan(finish):
        def scan_step(c, carry):
            lf = pl.multiple_of(c * CHUNK, CHUNK)
            lb = pl.multiple_of((chunks - 1 - c) * CHUNK, CHUNK)
            rows_f = pl.ds(lf, CHUNK)
            rows_b = pl.ds(lb, CHUNK)
            out_f = pl.ds(pl.multiple_of(s * tile + lf, CHUNK), CHUNK)
            out_b = pl.ds(pl.multiple_of((n_tiles - 1 - s) * tile + lb, CHUNK), CHUNK)
            res = []
            for h in range(n_heads):
                res.append(_hgrn_chunk(qf_ref[h, rows_f, :], zf_ref[h, rows_f, :], vf_ref[h, rows_f, :],
                                       lower_bound(lbf_ref, h), mf_ref[...], sgf_ref[...], stf_ref.at[h], False))
                res.append(_hgrn_chunk(qb_ref[h, rows_b, :], zb_ref[h, rows_b, :], vb_ref[h, rows_b, :],
                                       lower_bound(lbb_ref, h), mb_ref[...], sgb_ref[...], stb_ref.at[h], True))
            for h in range(n_heads):
                for o, rows, out_rows, g_ref in ((res[2 * h], rows_f, out_f, gf_ref),
                                                 (res[2 * h + 1], rows_b, out_b, gb_ref)):
                    if not finish:
                        part_ref[h, out_rows, :] = o
                        continue
                    o = o + part_ref[h, out_rows, :]
                    o = o * lax.rsqrt(jnp.mean(o * o, axis=-1, keepdims=True) + RMS_EPS)
                    gate = g_ref[h, rows, :]
                    o = o * ng_ref[h] * (gate * jax.nn.sigmoid(gate))
                    o_ref[out_rows, h * HEAD_DIM:(h + 1) * HEAD_DIM] = o.astype(BF16)
            return carry

        lax.fori_loop(0, chunks, scan_step, 0, unroll=HGRN_UNROLL)

    @pl.when(s < n_tiles // 2)
    def _():
        scan(finish=False)

    @pl.when(s >= n_tiles // 2)
    def _():
        scan(finish=True)


HGRN_HEADS_PER_STEP = 4
HGRN_TILE = 512
HGRN_UNROLL = 2


def _hgrn(hg, lb_fwd, lb_bwd, norm_g, batch, seq):
    hg5 = hg.reshape(N_HGRN_PARTS, N_HEADS_HGRN, batch, seq, HEAD_DIM)
    masks = _hgrn_masks()
    masks_f = jnp.asarray(masks, F32)
    masks_b = jnp.asarray(masks[:, ::-1, ::-1], F32)
    hps, tile = HGRN_HEADS_PER_STEP, HGRN_TILE
    n_tiles = seq // tile
    assert seq % tile == 0 and n_tiles % 2 == 0 and N_HEADS_HGRN % hps == 0

    def part(p, backward):
        def index(b, hg_, s):
            return (p, hg_, b, (n_tiles - 1 - s) if backward else s, 0)
        return pl.BlockSpec((None, hps, None, tile, HEAD_DIM), index)

    def per_head(rows):
        return pl.BlockSpec((hps, rows, HEAD_DIM), lambda b, hg_, s: (hg_, 0, 0))

    def lb_per_head(lb):
        return lb.reshape(DEPTH + 1, N_HEADS_HGRN, HEAD_DIM).transpose(1, 0, 2)

    const3 = lambda b, hg_, s: (0, 0, 0)
    q, zf, zb, v, g = range(N_HGRN_PARTS)
    return pl.pallas_call(
        functools.partial(_hgrn_kernel, n_tiles=n_tiles, tile=tile),
        grid=(batch, N_HEADS_HGRN // hps, n_tiles),
        in_specs=[
            part(q, False), part(zf, False), part(v, False), part(g, False),
            part(q, True), part(zb, True), part(v, True), part(g, True),
            per_head(DEPTH + 1), per_head(DEPTH + 1), per_head(1),
            pl.BlockSpec((N_LEVELS + 1, CHUNK, CHUNK), const3),
            pl.BlockSpec((N_LEVELS + 1, CHUNK, CHUNK), const3),
            pl.BlockSpec((N_LEVELS, CHUNK, HEAD_DIM), const3),
            pl.BlockSpec((N_LEVELS, CHUNK, HEAD_DIM), const3),
        ],
        out_specs=pl.BlockSpec((None, seq, hps * HEAD_DIM), lambda b, hg_, s: (b, 0, hg_)),
        out_shape=jax.ShapeDtypeStruct((batch, seq, HGRN_WIDTH), BF16),
        scratch_shapes=[
            pltpu.VMEM((hps, seq, HEAD_DIM), F32),
            pltpu.VMEM((hps, HEAD_DIM, HEAD_DIM), F32), pltpu.VMEM((hps, HEAD_DIM, HEAD_DIM), F32),
        ],
        compiler_params=pltpu.CompilerParams(
            dimension_semantics=("parallel", "parallel", "arbitrary"), vmem_limit_bytes=VMEM_LIMIT_BYTES),
        name="hgrn",
    )(hg5, hg5, hg5, hg5, hg5, hg5, hg5, hg5,
      lb_per_head(lb_fwd), lb_per_head(lb_bwd), norm_g.reshape(N_HEADS_HGRN, 1, HEAD_DIM),
      masks_f, masks_b, jnp.asarray(_hgrn_signs(False)), jnp.asarray(_hgrn_signs(True)))


def _attn_block_types(n_rows):
    row = np.arange(ATTN_BLOCK)

    def flags(n):
        i = n * ATTN_BLOCK + row
        reach = [ATTN_HALF * d // ATTN_CLASSES for d in ATTN_DILATIONS]
        return np.stack([i < r for r in reach] + [i >= n_rows - r for r in reach]).tobytes()

    starts = [0]
    for n in range(1, n_rows // ATTN_BLOCK):
        if flags(n) != flags(n - 1):
            starts.append(n)
    return tuple(starts)


def _attn_bias(delta, i_query, slope, n_rows):
    mult = jnp.zeros(delta.shape, F32)
    for d in ATTN_DILATIONS:
        reach = ATTN_HALF * d // ATTN_CLASSES
        lo = jnp.where(i_query < reach, 0, -ATTN_HALF * d)
        hi = jnp.where(i_query >= n_rows - reach, (ATTN_HALF - 1) * d, ATTN_HALF * d)
        hit = (delta >= lo) & (delta <= hi)
        if d > 1:
            hit = hit & (lax.rem(delta, d) == 0)
        mult = mult + hit.astype(F32)
    dist = jnp.abs(delta).astype(F32)
    return jnp.where(mult > 0.0, jnp.log(jnp.maximum(mult, 1.0)) - slope * dist, NEG_INF)


def _lane_reduce(parts, combine, reduce):
    folded = {}
    for part in parts:
        for j in range(0, part.shape[-1], LANES):
            tile = part[:, j:j + LANES]
            w = tile.shape[-1]
            folded[w] = tile if w not in folded else combine(folded[w], tile)
    out = None
    for tile in folded.values():
        r = reduce(tile, axis=-1, keepdims=True)
        out = r if out is None else combine(out, r)
    return out


def _attn_kernel(slope_ref, q_ref, k0_ref, k1_ref, k2_ref, k3_ref, v0_ref, v1_ref, v2_ref, v3_ref,
                 o_ref, kpad_ref, vpad_ref, bias_same_ref, bias_other_ref, *, n_rows, type_starts):
    b, rq = pl.program_id(1), pl.program_id(2)

    @pl.when((b == 0) & (rq == 0))
    def _():
        slope = slope_ref[...][:, 0:1]
        row = lax.broadcasted_iota(jnp.int32, (ATTN_BLOCK, ATTN_SAME), 0)
        col = lax.broadcasted_iota(jnp.int32, (ATTN_BLOCK, ATTN_SAME), 1)
        for ty, first_block in enumerate(type_starts):
            bias_same_ref[ty] = _attn_bias((col - ATTN_PAD - row) * ATTN_CLASSES,
                                           first_block * ATTN_BLOCK + row, slope, n_rows)
        row = lax.broadcasted_iota(jnp.int32, (ATTN_BLOCK, ATTN_OTHER), 0)
        col = lax.broadcasted_iota(jnp.int32, (ATTN_BLOCK, ATTN_OTHER), 1)
        for query_class in range(ATTN_CLASSES):
            for m in range(1, ATTN_CLASSES):
                delta = (col - ATTN_OTHER_LEAD - row) * ATTN_CLASSES + ((query_class + m) % ATTN_CLASSES - query_class)
                for ty, first_block in enumerate(type_starts):
                    bias_other_ref[query_class, ty, m - 1] = _attn_bias(
                        delta, first_block * ATTN_BLOCK + row, slope, n_rows)

    @pl.when(rq == 0)
    def _():
        zeros = jnp.zeros((ATTN_PAD, HEAD_DIM), BF16)
        for c, (k_ref, v_ref) in enumerate(((k0_ref, v0_ref), (k1_ref, v1_ref), (k2_ref, v2_ref), (k3_ref, v3_ref))):
            for pad_ref, src_ref in ((kpad_ref, k_ref), (vpad_ref, v_ref)):
                pad_ref[c, 0:ATTN_PAD, :] = zeros
                pad_ref[c, ATTN_PAD:ATTN_PAD + n_rows, :] = src_ref[...]
                pad_ref[c, ATTN_PAD + n_rows:ATTN_PAD + n_rows + ATTN_PAD, :] = zeros

    scale = HEAD_DIM ** -0.5

    def block(n, carry):
        i0 = pl.multiple_of(n * ATTN_BLOCK, ATTN_BLOCK)
        ty = jnp.int32(0)
        for first_block in type_starts[1:]:
            ty = ty + jnp.asarray(n >= first_block, jnp.int32)
        qb = q_ref[pl.ds(i0, ATTN_BLOCK), :]
        scores, values = [], []
        s = _dot_nt(qb, kpad_ref[rq, pl.ds(i0, ATTN_SAME), :]) * scale
        scores.append(s + bias_same_ref[ty])
        values.append(vpad_ref[rq, pl.ds(i0, ATTN_SAME), :])
        start = pl.multiple_of(i0 + (ATTN_PAD - ATTN_OTHER_LEAD), ATTN_OTHER_LEAD)
        for m in range(1, ATTN_CLASSES):
            rk = lax.rem(rq + m, ATTN_CLASSES)
            s = _dot_nt(qb, kpad_ref[rk, pl.ds(start, ATTN_OTHER), :]) * scale
            scores.append(s + bias_other_ref[rq, ty, m - 1])
            values.append(vpad_ref[rk, pl.ds(start, ATTN_OTHER), :])
        mx = _lane_reduce(scores, jnp.maximum, jnp.max)
        probs = [jnp.exp(s - mx) for s in scores]
        den = _lane_reduce(probs, jnp.add, jnp.sum)
        acc = jnp.zeros((ATTN_BLOCK, HEAD_DIM), F32)
        for p, v in zip(probs, values):
            acc = acc + jnp.dot(p.astype(BF16), v, preferred_element_type=F32)
        o_ref[pl.ds(i0, ATTN_BLOCK), :] = (acc / den).astype(BF16)
        return carry

    lax.fori_loop(0, n_rows // ATTN_BLOCK, block, 0, unroll=ATTN_UNROLL)


def _attn(qkv, batch, seq):
    n_rows = seq // ATTN_CLASSES
    assert seq % (ATTN_HALF * max(ATTN_DILATIONS)) == 0 and seq >= 2 * ATTN_HALF * max(ATTN_DILATIONS)
    assert n_rows % (ATTN_BLOCK * ATTN_UNROLL) == 0
    type_starts = _attn_block_types(n_rows)
    n_types = len(type_starts)
    qkv = qkv.reshape(N_ATTN_PARTS, batch, n_rows, ATTN_CLASSES * ATTN_WIDTH)
    slopes = 2.0 ** (-8.0 * (jnp.arange(N_HEADS_ATTN, dtype=F32) + 1.0) / N_HEADS_ATTN)
    slopes = jnp.broadcast_to(slopes[:, None, None], (N_HEADS_ATTN, 1, HEAD_DIM))

    def cls(part, c):
        return pl.BlockSpec((None, None, n_rows, HEAD_DIM), lambda h, b, r: (part, b, 0, c * N_HEADS_ATTN + h))

    padded = n_rows + 2 * ATTN_PAD
    out = pl.pallas_call(
        functools.partial(_attn_kernel, n_rows=n_rows, type_starts=type_starts),
        grid=(N_HEADS_ATTN, batch, ATTN_CLASSES),
        in_specs=[pl.BlockSpec((None, 1, HEAD_DIM), lambda h, b, r: (h, 0, 0)),
                  pl.BlockSpec((None, None, n_rows, HEAD_DIM), lambda h, b, r: (0, b, 0, r * N_HEADS_ATTN + h))]
                 + [cls(1, c) for c in range(ATTN_CLASSES)] + [cls(2, c) for c in range(ATTN_CLASSES)],
        out_specs=pl.BlockSpec((None, n_rows, HEAD_DIM), lambda h, b, r: (b, 0, r * N_HEADS_ATTN + h)),
        out_shape=jax.ShapeDtypeStruct((batch, n_rows, ATTN_CLASSES * ATTN_WIDTH), BF16),
        scratch_shapes=[
            pltpu.VMEM((ATTN_CLASSES, padded, HEAD_DIM), BF16),
            pltpu.VMEM((ATTN_CLASSES, padded, HEAD_DIM), BF16),
            pltpu.VMEM((n_types, ATTN_BLOCK, ATTN_SAME), F32),
            pltpu.VMEM((ATTN_CLASSES, n_types, ATTN_CLASSES - 1, ATTN_BLOCK, ATTN_OTHER), F32),
        ],
        compiler_params=pltpu.CompilerParams(
            dimension_semantics=("arbitrary", "arbitrary", "arbitrary"), vmem_limit_bytes=VMEM_LIMIT_BYTES),
        name="attn",
    )(slopes, *([qkv] * (1 + 2 * ATTN_CLASSES)))
    return out.reshape(batch * n_rows, ATTN_CLASSES * ATTN_WIDTH)


def _out_proj_kernel(hg_ref, at_ref, w_ref, x_ref, g_ref, b_ref, o_ref, y_ref):
    _store_lane_tiles(y_ref, ALPHA * x_ref[...]
                      + jnp.dot(hg_ref[...], w_ref[0:HGRN_WIDTH, :], preferred_element_type=F32))
    rows = at_ref.shape[0]
    n_tiles = y_ref.shape[0]
    for c in range(ATTN_CLASSES):
        mix = jnp.dot(at_ref[:, c * ATTN_WIDTH:(c + 1) * ATTN_WIDTH], w_ref[HGRN_WIDTH:HGRN_WIDTH + ATTN_WIDTH, :],
                      preferred_element_type=F32)
        cls_rows = pl.ds(c, rows, stride=ATTN_CLASSES)
        y = jnp.concatenate([y_ref[j, cls_rows, :] for j in range(n_tiles)], axis=-1) + mix
        y = _layer_norm(y, g_ref[...], b_ref[...])
        for j in range(n_tiles):
            y_ref[j, cls_rows, :] = y[:, j * LANES:(j + 1) * LANES]
    o_ref[...] = jnp.concatenate([y_ref[j] for j in range(n_tiles)], axis=-1)


def _out_proj(hg, at4, w_out, x, g, b, *, tm=512):
    t, d = x.shape
    tok = lambda i: (i, 0)
    const = lambda i: (0, 0)
    return pl.pallas_call(
        _out_proj_kernel,
        grid=(t // tm,),
        in_specs=[
            pl.BlockSpec((tm, HGRN_WIDTH), tok),
            pl.BlockSpec((tm // ATTN_CLASSES, ATTN_CLASSES * ATTN_WIDTH), tok),
            pl.BlockSpec((HGRN_WIDTH + ATTN_WIDTH, d), const),
            pl.BlockSpec((tm, d), tok),
            pl.BlockSpec((1, d), const),
            pl.BlockSpec((1, d), const),
        ],
        out_specs=pl.BlockSpec((tm, d), tok),
        out_shape=jax.ShapeDtypeStruct((t, d), F32),
        scratch_shapes=[pltpu.VMEM((d // LANES, tm, LANES), F32)],
        compiler_params=pltpu.CompilerParams(
            dimension_semantics=("parallel",), vmem_limit_bytes=VMEM_LIMIT_BYTES),
        name="out_proj",
    )(hg, at4, w_out, x, g, b)


def _encoder_layer(x, w, layer):
    batch, seq, d = x.shape
    h = x.reshape(batch * seq, d)
    h, hb = _ffn(h, w["ffn1_w_gate"][layer], w["ffn1_w_up"][layer], w["ffn1_w_down"][layer],
                 w["ln1_g"][layer:layer + 1], w["ln1_b"][layer:layer + 1], emit_bf16=True)
    hg, qkv = _in_proj(hb, w["w_in"][layer])
    hgrn_out = _hgrn(hg, w["hgrn_lb_fwd"], w["hgrn_lb_bwd"], w["hgrn_norm_g"][layer], batch, seq)
    attn_out = _attn(qkv, batch, seq)
    h = _out_proj(hgrn_out.reshape(batch * seq, HGRN_WIDTH), attn_out, w["w_out"][layer], h,
                  w["ln2_g"][layer:layer + 1], w["ln2_b"][layer:layer + 1])
    h, = _ffn(h, w["ffn2_w_gate"][layer], w["ffn2_w_up"][layer], w["ffn2_w_down"][layer],
              w["ln3_g"][layer:layer + 1], w["ln3_b"][layer:layer + 1], emit_bf16=False)
    return h.reshape(batch, seq, d)


_MATMUL_WEIGHTS = ("ffn1_w_gate", "ffn1_w_up", "ffn1_w_down", "w_in", "w_out",
                   "ffn2_w_gate", "ffn2_w_up", "ffn2_w_down")


def kernel(x_prompt, x_sample, ln1_g, ln1_b, ffn1_w_gate, ffn1_w_up, ffn1_w_down, ln2_g, ln2_b, w_in,
           hgrn_lb_fwd, hgrn_lb_bwd, hgrn_norm_g, w_out, ln3_g, ln3_b, ffn2_w_gate, ffn2_w_up, ffn2_w_down):
    w = dict(ln1_g=ln1_g, ln1_b=ln1_b, ffn1_w_gate=ffn1_w_gate, ffn1_w_up=ffn1_w_up, ffn1_w_down=ffn1_w_down,
             ln2_g=ln2_g, ln2_b=ln2_b, w_in=w_in, hgrn_lb_fwd=hgrn_lb_fwd, hgrn_lb_bwd=hgrn_lb_bwd,
             hgrn_norm_g=hgrn_norm_g, w_out=w_out, ln3_g=ln3_g, ln3_b=ln3_b,
             ffn2_w_gate=ffn2_w_gate, ffn2_w_up=ffn2_w_up, ffn2_w_down=ffn2_w_down)
    for name in _MATMUL_WEIGHTS:
        w[name] = w[name].astype(BF16)
    outs = []
    for x in (x_prompt, x_sample):
        for layer in range(DEPTH):
            x = _encoder_layer(x, w, layer)
        outs.append(x)
    return tuple(outs)
```

```python
import functools

import numpy as np
import jax
import jax.numpy as jnp
from jax import lax
from jax.experimental import pallas as pl
from jax.experimental.pallas import tpu as pltpu

F32 = jnp.float32
BF16 = jnp.bfloat16

D_MODEL = 2048
D_FF = 5632
DEPTH = 1
HEAD_DIM = 128
N_HEADS_HGRN = 8
N_HEADS_ATTN = 8
HGRN_WIDTH = N_HEADS_HGRN * HEAD_DIM
ATTN_WIDTH = N_HEADS_ATTN * HEAD_DIM
IN_WIDTH = 5 * HGRN_WIDTH + 3 * ATTN_WIDTH
ALPHA = (2 * DEPTH) ** 0.25
LN_EPS = 1e-5
RMS_EPS = 1e-6
NEG_INF = -1e30

CHUNK = 64
HGRN_LEVELS = (1, 2, 4, 8, 16, 32)
N_LEVELS = len(HGRN_LEVELS)

ATTN_HALF = 64
ATTN_DILATIONS = (1, 4, 16)
ATTN_CLASSES = 4
ATTN_BLOCK = 128
ATTN_PAD = 256
ATTN_SAME = ATTN_BLOCK + 2 * ATTN_PAD
ATTN_OTHER_LEAD = 32
ATTN_OTHER = ATTN_BLOCK + 2 * ATTN_OTHER_LEAD
ATTN_UNROLL = 4
LANES = 128
SUBLANES = 8

VMEM_LIMIT_BYTES = 56 * 1024 * 1024


def _layer_norm(y, g, b):
    mu = jnp.mean(y, axis=-1, keepdims=True)
    yc = y - mu
    var = jnp.mean(yc * yc, axis=-1, keepdims=True)
    return yc * lax.rsqrt(var + LN_EPS) * g + b


def _dot_nt(a, b):
    return lax.dot_general(a, b, (((1,), (1,)), ((), ())), preferred_element_type=F32)


def _dot_tn(a, b):
    return lax.dot_general(a, b, (((0,), (0,)), ((), ())), preferred_element_type=F32)


def _ffn_kernel(x_ref, wg_ref, wu_ref, wd_ref, g_ref, b_ref, *refs, n_ff_tiles, emit_bf16):
    if emit_bf16:
        o_ref, ob_ref, xb_ref, h_ref, acc_ref = refs
    else:
        o_ref, xb_ref, h_ref, acc_ref = refs
    j = pl.program_id(1)

    def hidden_tile(xb):
        gate = jnp.dot(xb, wg_ref[...], preferred_element_type=F32)
        up = jnp.dot(xb, wu_ref[...], preferred_element_type=F32)
        return (gate * jax.nn.sigmoid(gate) * up).astype(BF16)

    def down_tile():
        return jnp.dot(h_ref[...], wd_ref[...], preferred_element_type=F32)

    @pl.when(j == 0)
    def _():
        xb = x_ref[...].astype(BF16)
        xb_ref[...] = xb
        h_ref[...] = hidden_tile(xb)

    @pl.when(j == 1)
    def _():
        acc_ref[...] = down_tile()
        h_ref[...] = hidden_tile(xb_ref[...])

    @pl.when((j > 1) & (j < n_ff_tiles))
    def _():
        acc_ref[...] += down_tile()
        h_ref[...] = hidden_tile(xb_ref[...])

    @pl.when(j == n_ff_tiles)
    def _():
        y = _layer_norm(ALPHA * x_ref[...] + 0.5 * (acc_ref[...] + down_tile()), g_ref[...], b_ref[...])
        o_ref[...] = y
        if emit_bf16:
            ob_ref[...] = y.astype(BF16)


def _ffn(x, wg, wu, wd, g, b, *, emit_bf16, tm=512, tf=512):
    t, d = x.shape
    n_ff_tiles = D_FF // tf
    tok = lambda i, j: (i, 0)
    out_specs = [pl.BlockSpec((tm, d), tok)]
    out_shape = [jax.ShapeDtypeStruct((t, d), F32)]
    if emit_bf16:
        out_specs.append(pl.BlockSpec((tm, d), tok))
        out_shape.append(jax.ShapeDtypeStruct((t, d), BF16))
    return pl.pallas_call(
        functools.partial(_ffn_kernel, n_ff_tiles=n_ff_tiles, emit_bf16=emit_bf16),
        grid=(t // tm, n_ff_tiles + 1),
        in_specs=[
            pl.BlockSpec((tm, d), tok),
            pl.BlockSpec((d, tf), lambda i, j: (0, jnp.minimum(j, n_ff_tiles - 1))),
            pl.BlockSpec((d, tf), lambda i, j: (0, jnp.minimum(j, n_ff_tiles - 1))),
            pl.BlockSpec((tf, d), lambda i, j: (jnp.maximum(j - 1, 0), 0)),
            pl.BlockSpec((1, d), lambda i, j: (0, 0)),
            pl.BlockSpec((1, d), lambda i, j: (0, 0)),
        ],
        out_specs=out_specs,
        out_shape=out_shape,
        scratch_shapes=[pltpu.VMEM((tm, d), BF16), pltpu.VMEM((tm, tf), BF16), pltpu.VMEM((tm, d), F32)],
        compiler_params=pltpu.CompilerParams(
            dimension_semantics=("parallel", "arbitrary"), vmem_limit_bytes=VMEM_LIMIT_BYTES),
        name="ffn",
    )(x, wg, wu, wd, g, b)


N_HGRN_PARTS = 5
N_ATTN_PARTS = 3
IN_TILE = 1024


def _proj_heads_kernel(x_ref, w_ref, o_ref):
    r = jnp.dot(x_ref[...], w_ref[...], preferred_element_type=F32)
    for c in range(N_HEADS_HGRN):
        o_ref[c] = r[:, c * HEAD_DIM:(c + 1) * HEAD_DIM]


def _store_lane_tiles(ref, x):
    for j in range(ref.shape[0]):
        ref[j] = x[:, j * LANES:(j + 1) * LANES]


def _proj_classes_kernel(x_ref, w_ref, o_ref, r_ref):
    _store_lane_tiles(r_ref, jnp.dot(x_ref[...], w_ref[...], preferred_element_type=F32))
    rows = o_ref.shape[0]
    for c in range(ATTN_CLASSES):
        cls_rows = pl.ds(c, rows, stride=ATTN_CLASSES)
        for j in range(r_ref.shape[0]):
            lo = c * ATTN_WIDTH + j * LANES
            o_ref[:, lo:lo + LANES] = r_ref[j, cls_rows, :].astype(BF16)


def _in_proj(xb, w_in, *, tm=1024):
    t, d = xb.shape
    params = pltpu.CompilerParams(dimension_semantics=("parallel", "arbitrary"), vmem_limit_bytes=VMEM_LIMIT_BYTES)
    hg = pl.pallas_call(
        _proj_heads_kernel,
        grid=(t // tm, N_HGRN_PARTS),
        in_specs=[pl.BlockSpec((tm, d), lambda i, j: (i, 0)), pl.BlockSpec((d, IN_TILE), lambda i, j: (0, j))],
        out_specs=pl.BlockSpec((N_HEADS_HGRN, tm, HEAD_DIM), lambda i, j: (j, i, 0)),
        out_shape=jax.ShapeDtypeStruct((N_HGRN_PARTS * N_HEADS_HGRN, t, HEAD_DIM), F32),
        compiler_params=params,
        name="in_proj_hgrn",
    )(xb, w_in)
    qkv = pl.pallas_call(
        _proj_classes_kernel,
        grid=(t // tm, N_ATTN_PARTS),
        in_specs=[pl.BlockSpec((tm, d), lambda i, j: (i, 0)),
                  pl.BlockSpec((d, IN_TILE), lambda i, j: (0, N_HGRN_PARTS + j))],
        out_specs=pl.BlockSpec((None, tm // ATTN_CLASSES, ATTN_CLASSES * ATTN_WIDTH), lambda i, j: (j, i, 0)),
        out_shape=jax.ShapeDtypeStruct((N_ATTN_PARTS, t // ATTN_CLASSES, ATTN_CLASSES * ATTN_WIDTH), BF16),
        scratch_shapes=[pltpu.VMEM((IN_TILE // LANES, tm, LANES), F32)],
        compiler_params=params,
        name="in_proj_attn",
    )(xb, w_in)
    return hg, qkv


def _hgrn_masks():
    t = np.arange(CHUNK)[:, None]
    u = np.arange(CHUNK)[None, :]
    masks = [(t == u)]
    for c in HGRN_LEVELS:
        same = (t // (2 * c)) == (u // (2 * c))
        masks.append(same & ((t % (2 * c)) >= c) & ((u % (2 * c)) < c))
    return np.stack(masks, axis=0).astype(np.float32)


def _rows(x, r, n):
    return jnp.broadcast_to(x[r:r + 1], (n, x.shape[-1]))


def _chunk_cumsum(x, reverse):
    sub = lax.broadcasted_iota(jnp.int32, x.shape, 0) % SUBLANES
    step = 1
    while step < SUBLANES:
        if reverse:
            x = x + jnp.where(sub < SUBLANES - step, pltpu.roll(x, CHUNK - step, axis=0), 0.0)
        else:
            x = x + jnp.where(sub >= step, pltpu.roll(x, step, axis=0), 0.0)
        step *= 2
    tiles = [x[j:j + SUBLANES] for j in range(0, CHUNK, SUBLANES)]
    if reverse:
        tiles = tiles[::-1]
    edge = 0 if reverse else SUBLANES - 1
    for j in range(1, len(tiles)):
        tiles[j] = tiles[j] + _rows(tiles[j - 1], edge, SUBLANES)
    if reverse:
        tiles = tiles[::-1]
    return jnp.concatenate(tiles, axis=0)


def _query_side(c, reverse):
    t = np.arange(CHUNK)
    return ((t % (2 * c)) >= c) != reverse


def _hgrn_signs(reverse):
    log2e = np.float32(np.log2(np.e))
    sign = np.stack([np.where(_query_side(c, reverse), log2e, -log2e) for c in HGRN_LEVELS])
    return np.broadcast_to(sign[:, :, None], (N_LEVELS, CHUNK, HEAD_DIM)).astype(np.float32)


def _level_boundary(g, c, reverse):
    first = c if reverse else c - 1
    if 2 * c >= SUBLANES:
        return jnp.concatenate([_rows(g, b, 2 * c) for b in range(first, CHUNK, 2 * c)], axis=0)
    sub = lax.broadcasted_iota(jnp.int32, (SUBLANES, g.shape[-1]), 0)
    return jnp.concatenate(
        [jnp.where(sub < 2 * c, _rows(g, j + first, SUBLANES), _rows(g, j + 2 * c + first, SUBLANES))
         for j in range(0, CHUNK, SUBLANES)], axis=0)


def _hgrn_chunk(q, z, v, lb, masks, signs, state_ref, reverse):
    f = lb + (1.0 - lb) * jax.nn.sigmoid(z)
    k = 1.0 - f
    g_incl = _chunk_cumsum(jnp.log(f), reverse)
    last_row = 0 if reverse else CHUNK - 1
    decay_in = jnp.exp(g_incl)
    decay_out = jnp.exp(_rows(g_incl, last_row, CHUNK) - g_incl)
    decay_all = jnp.exp(g_incl[last_row:last_row + 1])

    qb, kb, vb = q.astype(BF16), k.astype(BF16), v.astype(BF16)
    diag = _dot_nt(qb, kb) * masks[0]
    a = [diag[j:j + SUBLANES] for j in range(0, CHUNK, SUBLANES)]
    sub = lax.broadcasted_iota(jnp.int32, q.shape, 0)
    for lv, c in enumerate(HGRN_LEVELS):
        mask = masks[lv + 1]
        query_side = _query_side(c, reverse)
        if c < SUBLANES:
            rows_q = ((sub % (2 * c)) >= c) != reverse
            if c == 1:
                mixed = jnp.where(rows_q, q * f, k)
            else:
                decay = jnp.exp2((g_incl - _level_boundary(g_incl, c, reverse)) * signs[lv])
                mixed = jnp.where(rows_q, q, k) * decay
            mixed = mixed.astype(BF16)
            prod = _dot_nt(mixed, mixed) * mask
            a = [a[j] + prod[j * SUBLANES:(j + 1) * SUBLANES] for j in range(len(a))]
            continue
        decay = jnp.exp2((g_incl - _level_boundary(g_incl, c, reverse)) * signs[lv])
        mixed = jnp.concatenate([(q if query_side[r] else k)[r:r + c] for r in range(0, CHUNK, c)], axis=0) * decay
        q_rows = [r for r in range(0, CHUNK, SUBLANES) if query_side[r]]
        lhs = jnp.concatenate([mixed[r:r + SUBLANES] for r in q_rows], axis=0).astype(BF16)
        prod = _dot_nt(lhs, mixed.astype(BF16))
        for i, r in enumerate(q_rows):
            j = r // SUBLANES
            a[j] = a[j] + prod[i * SUBLANES:(i + 1) * SUBLANES] * mask[r:r + SUBLANES]
    a = jnp.concatenate(a, axis=0)
    state = state_ref[...]
    out = jnp.dot(a.astype(BF16), vb, preferred_element_type=F32)
    out = out + _dot_nt((q * decay_in).astype(BF16), state.astype(BF16))
    state_ref[...] = state * decay_all + _dot_tn(vb, (k * decay_out).astype(BF16))
    return out


def _hgrn_kernel(qf_ref, zf_ref, vf_ref, gf_ref, qb_ref, zb_ref, vb_ref, gb_ref, lbf_ref, lbb_ref, ng_ref,
                 mf_ref, mb_ref, sgf_ref, sgb_ref, o_ref, part_ref, stf_ref, stb_ref, *, n_tiles, tile):
    s = pl.program_id(2)
    n_heads = qf_ref.shape[0]
    chunks = tile // CHUNK

    @pl.when(s == 0)
    def _():
        stf_ref[...] = jnp.zeros_like(stf_ref)
        stb_ref[...] = jnp.zeros_like(stb_ref)

    def lower_bound(ref, h):
        p = ref[h]
        p = jnp.exp(p - jnp.max(p, axis=0, keepdims=True))
        return p[0:1] / jnp.sum(p, axis=0, keepdims=True)

    def scan(finish):
        def scan_step(c, carry):
            lf = pl.multiple_of(c * CHUNK, CHUNK)
            lb = pl.multiple_of((chunks - 1 - c) * CHUNK, CHUNK)
            rows_f = pl.ds(lf, CHUNK)
            rows_b = pl.ds(lb, CHUNK)
            out_f = pl.ds(pl.multiple_of(s * tile + lf, CHUNK), CHUNK)
            out_b = pl.ds(pl.multiple_of((n_tiles - 1 - s) * tile + lb, CHUNK), CHUNK)
            res = []
            for h in range(n_heads):
                res.append(_hgrn_chunk(qf_ref[h, rows_f, :], zf_ref[h, rows_f, :], vf_ref[h, rows_f, :],
                                       lower_bound(lbf_ref, h), mf_ref[...], sgf_ref[...], stf_ref.at[h], False))
                res.append(_hgrn_chunk(qb_ref[h, rows_b, :], zb_ref[h, rows_b, :], vb_ref[h, rows_b, :],
                                       lower_bound(lbb_ref, h), mb_ref[...], sgb_ref[...], stb_ref.at[h], True))
            for h in range(n_heads):
                for o, rows, out_rows, g_ref in ((res[2 * h], rows_f, out_f, gf_ref),
                                                 (res[2 * h + 1], rows_b, out_b, gb_ref)):
                    if not finish:
                        part_ref[h, out_rows, :] = o
                        continue
                    o = o + part_ref[h, out_rows, :]
                    o = o * lax.rsqrt(jnp.mean(o * o, axis=-1, keepdims=True) + RMS_EPS)
                    gate = g_ref[h, rows, :]
                    o = o * ng_ref[h] * (gate * jax.nn.sigmoid(gate))
                    o_ref[out_rows, h * HEAD_DIM:(h + 1) * HEAD_DIM] = o.astype(BF16)
            return carry

        lax.fori_loop(0, chunks, scan_step, 0, unroll=HGRN_UNROLL)

    @pl.when(s < n_tiles // 2)
    def _():
        scan(finish=False)

    @pl.when(s >= n_tiles // 2)
    def _():
        scan(finish=True)


HGRN_HEADS_PER_STEP = 4
HGRN_TILE = 512
HGRN_UNROLL = 2


def _hgrn(hg, lb_fwd, lb_bwd, norm_g, batch, seq):
    hg5 = hg.reshape(N_HGRN_PARTS, N_HEADS_HGRN, batch, seq, HEAD_DIM)
    masks = _hgrn_masks()
    masks_f = jnp.asarray(masks, F32)
    masks_b = jnp.asarray(masks[:, ::-1, ::-1], F32)
    hps, tile = HGRN_HEADS_PER_STEP, HGRN_TILE
    n_tiles = seq // tile
    assert seq % tile == 0 and n_tiles % 2 == 0 and N_HEADS_HGRN % hps == 0

    def part(p, backward):
        def index(b, hg_, s):
            return (p, hg_, b, (n_tiles - 1 - s) if backward else s, 0)
        return pl.BlockSpec((None, hps, None, tile, HEAD_DIM), index)

    def per_head(rows):
        return pl.BlockSpec((hps, rows, HEAD_DIM), lambda b, hg_, s: (hg_, 0, 0))

    def lb_per_head(lb):
        return lb.reshape(DEPTH + 1, N_HEADS_HGRN, HEAD_DIM).transpose(1, 0, 2)

    const3 = lambda b, hg_, s: (0, 0, 0)
    q, zf, zb, v, g = range(N_HGRN_PARTS)
    return pl.pallas_call(
        functools.partial(_hgrn_kernel, n_tiles=n_tiles, tile=tile),
        grid=(batch, N_HEADS_HGRN // hps, n_tiles),
        in_specs=[
            part(q, False), part(zf, False), part(v, False), part(g, False),
            part(q, True), part(zb, True), part(v, True), part(g, True),
            per_head(DEPTH + 1), per_head(DEPTH + 1), per_head(1),
            pl.BlockSpec((N_LEVELS + 1, CHUNK, CHUNK), const3),
            pl.BlockSpec((N_LEVELS + 1, CHUNK, CHUNK), const3),
            pl.BlockSpec((N_LEVELS, CHUNK, HEAD_DIM), const3),
            pl.BlockSpec((N_LEVELS, CHUNK, HEAD_DIM), const3),
        ],
        out_specs=pl.BlockSpec((None, seq, hps * HEAD_DIM), lambda b, hg_, s: (b, 0, hg_)),
        out_shape=jax.ShapeDtypeStruct((batch, seq, HGRN_WIDTH), BF16),
        scratch_shapes=[
            pltpu.VMEM((hps, seq, HEAD_DIM), F32),
            pltpu.VMEM((hps, HEAD_DIM, HEAD_DIM), F32), pltpu.VMEM((hps, HEAD_DIM, HEAD_DIM), F32),
        ],
        compiler_params=pltpu.CompilerParams(
            dimension_semantics=("parallel", "parallel", "arbitrary"), vmem_limit_bytes=VMEM_LIMIT_BYTES),
        name="hgrn",
    )(hg5, hg5, hg5, hg5, hg5, hg5, hg5, hg5,
      lb_per_head(lb_fwd), lb_per_head(lb_bwd), norm_g.reshape(N_HEADS_HGRN, 1, HEAD_DIM),
      masks_f, masks_b, jnp.asarray(_hgrn_signs(False)), jnp.asarray(_hgrn_signs(True)))


def _attn_block_types(n_rows):
    row = np.arange(ATTN_BLOCK)

    def flags(n):
        i = n * ATTN_BLOCK + row
        reach = [ATTN_HALF * d // ATTN_CLASSES for d in ATTN_DILATIONS]
        return np.stack([i < r for r in reach] + [i >= n_rows - r for r in reach]).tobytes()

    starts = [0]
    for n in range(1, n_rows // ATTN_BLOCK):
        if flags(n) != flags(n - 1):
            starts.append(n)
    return tuple(starts)


def _attn_bias(delta, i_query, slope, n_rows):
    mult = jnp.zeros(delta.shape, F32)
    for d in ATTN_DILATIONS:
        reach = ATTN_HALF * d // ATTN_CLASSES
        lo = jnp.where(i_query < reach, 0, -ATTN_HALF * d)
        hi = jnp.where(i_query >= n_rows - reach, (ATTN_HALF - 1) * d, ATTN_HALF * d)
        hit = (delta >= lo) & (delta <= hi)
        if d > 1:
            hit = hit & (lax.rem(delta, d) == 0)
        mult = mult + hit.astype(F32)
    dist = jnp.abs(delta).astype(F32)
    return jnp.where(mult > 0.0, jnp.log(jnp.maximum(mult, 1.0)) - slope * dist, NEG_INF)


def _lane_reduce(parts, combine, reduce):
    folded = {}
    for part in parts:
        for j in range(0, part.shape[-1], LANES):
            tile = part[:, j:j + LANES]
            w = tile.shape[-1]
            folded[w] = tile if w not in folded else combine(folded[w], tile)
    out = None
    for tile in folded.values():
        r = reduce(tile, axis=-1, keepdims=True)
        out = r if out is None else combine(out, r)
    return out


def _attn_kernel(slope_ref, q_ref, k0_ref, k1_ref, k2_ref, k3_ref, v0_ref, v1_ref, v2_ref, v3_ref,
                 o_ref, kpad_ref, vpad_ref, bias_same_ref, bias_other_ref, *, n_rows, type_starts):
    b, rq = pl.program_id(1), pl.program_id(2)

    @pl.when((b == 0) & (rq == 0))
    def _():
        slope = slope_ref[...][:, 0:1]
        row = lax.broadcasted_iota(jnp.int32, (ATTN_BLOCK, ATTN_SAME), 0)
        col = lax.broadcasted_iota(jnp.int32, (ATTN_BLOCK, ATTN_SAME), 1)
        for ty, first_block in enumerate(type_starts):
            bias_same_ref[ty] = _attn_bias((col - ATTN_PAD - row) * ATTN_CLASSES,
                                           first_block * ATTN_BLOCK + row, slope, n_rows)
        row = lax.broadcasted_iota(jnp.int32, (ATTN_BLOCK, ATTN_OTHER), 0)
        col = lax.broadcasted_iota(jnp.int32, (ATTN_BLOCK, ATTN_OTHER), 1)
        for query_class in range(ATTN_CLASSES):
            for m in range(1, ATTN_CLASSES):
                delta = (col - ATTN_OTHER_LEAD - row) * ATTN_CLASSES + ((query_class + m) % ATTN_CLASSES - query_class)
                for ty, first_block in enumerate(type_starts):
                    bias_other_ref[query_class, ty, m - 1] = _attn_bias(
                        delta, first_block * ATTN_BLOCK + row, slope, n_rows)

    @pl.when(rq == 0)
    def _():
        zeros = jnp.zeros((ATTN_PAD, HEAD_DIM), BF16)
        for c, (k_ref, v_ref) in enumerate(((k0_ref, v0_ref), (k1_ref, v1_ref), (k2_ref, v2_ref), (k3_ref, v3_ref))):
            for pad_ref, src_ref in ((kpad_ref, k_ref), (vpad_ref, v_ref)):
                pad_ref[c, 0:ATTN_PAD, :] = zeros
                pad_ref[c, ATTN_PAD:ATTN_PAD + n_rows, :] = src_ref[...]
                pad_ref[c, ATTN_PAD + n_rows:ATTN_PAD + n_rows + ATTN_PAD, :] = zeros

    scale = HEAD_DIM ** -0.5

    def block(n, carry):
        i0 = pl.multiple_of(n * ATTN_BLOCK, ATTN_BLOCK)
        ty = jnp.int32(0)
        for first_block in type_starts[1:]:
            ty = ty + jnp.asarray(n >= first_block, jnp.int32)
        qb = q_ref[pl.ds(i0, ATTN_BLOCK), :]
        scores, values = [], []
        s = _dot_nt(qb, kpad_ref[rq, pl.ds(i0, ATTN_SAME), :]) * scale
        scores.append(s + bias_same_ref[ty])
        values.append(vpad_ref[rq, pl.ds(i0, ATTN_SAME), :])
        start = pl.multiple_of(i0 + (ATTN_PAD - ATTN_OTHER_LEAD), ATTN_OTHER_LEAD)
        for m in range(1, ATTN_CLASSES):
            rk = lax.rem(rq + m, ATTN_CLASSES)
            s = _dot_nt(qb, kpad_ref[rk, pl.ds(start, ATTN_OTHER), :]) * scale
            scores.append(s + bias_other_ref[rq, ty, m - 1])
            values.append(vpad_ref[rk, pl.ds(start, ATTN_OTHER), :])
        mx = _lane_reduce(scores, jnp.maximum, jnp.max)
        probs = [jnp.exp(s - mx) for s in scores]
        den = _lane_reduce(probs, jnp.add, jnp.sum)
        acc = jnp.zeros((ATTN_BLOCK, HEAD_DIM), F32)
        for p, v in zip(probs, values):
            acc = acc + jnp.dot(p.astype(BF16), v, preferred_element_type=F32)
        o_ref[pl.ds(i0, ATTN_BLOCK), :] = (acc / den).astype(BF16)
        return carry

    lax.fori_loop(0, n_rows // ATTN_BLOCK, block, 0, unroll=ATTN_UNROLL)


def _attn(qkv, batch, seq):
    n_rows = seq // ATTN_CLASSES
    assert seq % (ATTN_HALF * max(ATTN_DILATIONS)) == 0 and seq >= 2 * ATTN_HALF * max(ATTN_DILATIONS)
    assert n_rows % (ATTN_BLOCK * ATTN_UNROLL) == 0
    type_starts = _attn_block_types(n_rows)
    n_types = len(type_starts)
    qkv = qkv.reshape(N_ATTN_PARTS, batch, n_rows, ATTN_CLASSES * ATTN_WIDTH)
    slopes = 2.0 ** (-8.0 * (jnp.arange(N_HEADS_ATTN, dtype=F32) + 1.0) / N_HEADS_ATTN)
    slopes = jnp.broadcast_to(slopes[:, None, None], (N_HEADS_ATTN, 1, HEAD_DIM))

    def cls(part, c):
        return pl.BlockSpec((None, None, n_rows, HEAD_DIM), lambda h, b, r: (part, b, 0, c * N_HEADS_ATTN + h))

    padded = n_rows + 2 * ATTN_PAD
    out = pl.pallas_call(
        functools.partial(_attn_kernel, n_rows=n_rows, type_starts=type_starts),
        grid=(N_HEADS_ATTN, batch, ATTN_CLASSES),
        in_specs=[pl.BlockSpec((None, 1, HEAD_DIM), lambda h, b, r: (h, 0, 0)),
                  pl.BlockSpec((None, None, n_rows, HEAD_DIM), lambda h, b, r: (0, b, 0, r * N_HEADS_ATTN + h))]
                 + [cls(1, c) for c in range(ATTN_CLASSES)] + [cls(2, c) for c in range(ATTN_CLASSES)],
        out_specs=pl.BlockSpec((None, n_rows, HEAD_DIM), lambda h, b, r: (b, 0, r * N_HEADS_ATTN + h)),
        out_shape=jax.ShapeDtypeStruct((batch, n_rows, ATTN_CLASSES * ATTN_WIDTH), BF16),
        scratch_shapes=[
            pltpu.VMEM((ATTN_CLASSES, padded, HEAD_DIM), BF16),
            pltpu.VMEM((ATTN_CLASSES, padded, HEAD_DIM), BF16),
            pltpu.VMEM((n_types, ATTN_BLOCK, ATTN_SAME), F32),
            pltpu.VMEM((ATTN_CLASSES, n_types, ATTN_CLASSES - 1, ATTN_BLOCK, ATTN_OTHER), F32),
        ],
        compiler_params=pltpu.CompilerParams(
            dimension_semantics=("arbitrary", "arbitrary", "arbitrary"), vmem_limit_bytes=VMEM_LIMIT_BYTES),
        name="attn",
    )(slopes, *([qkv] * (1 + 2 * ATTN_CLASSES)))
    return out.reshape(batch * n_rows, ATTN_CLASSES * ATTN_WIDTH)


def _out_proj_kernel(hg_ref, at_ref, w_ref, x_ref, g_ref, b_ref, o_ref, y_ref):
    _store_lane_tiles(y_ref, ALPHA * x_ref[...]
                      + jnp.dot(hg_ref[...], w_ref[0:HGRN_WIDTH, :], preferred_element_type=F32))
    rows = at_ref.shape[0]
    n_tiles = y_ref.shape[0]
    for c in range(ATTN_CLASSES):
        mix = jnp.dot(at_ref[:, c * ATTN_WIDTH:(c + 1) * ATTN_WIDTH], w_ref[HGRN_WIDTH:HGRN_WIDTH + ATTN_WIDTH, :],
                      preferred_element_type=F32)
        cls_rows = pl.ds(c, rows, stride=ATTN_CLASSES)
        y = jnp.concatenate([y_ref[j, cls_rows, :] for j in range(n_tiles)], axis=-1) + mix
        y = _layer_norm(y, g_ref[...], b_ref[...])
        for j in range(n_tiles):
            y_ref[j, cls_rows, :] = y[:, j * LANES:(j + 1) * LANES]
    o_ref[...] = jnp.concatenate([y_ref[j] for j in range(n_tiles)], axis=-1)


def _out_proj(hg, at4, w_out, x, g, b, *, tm=512):
    t, d = x.shape
    tok = lambda i: (i, 0)
    const = lambda i: (0, 0)
    return pl.pallas_call(
        _out_proj_kernel,
        grid=(t // tm,),
        in_specs=[
            pl.BlockSpec((tm, HGRN_WIDTH), tok),
            pl.BlockSpec((tm // ATTN_CLASSES, ATTN_CLASSES * ATTN_WIDTH), tok),
            pl.BlockSpec((HGRN_WIDTH + ATTN_WIDTH, d), const),
            pl.BlockSpec((tm, d), tok),
            pl.BlockSpec((1, d), const),
            pl.BlockSpec((1, d), const),
        ],
        out_specs=pl.BlockSpec((tm, d), tok),
        out_shape=jax.ShapeDtypeStruct((t, d), F32),
        scratch_shapes=[pltpu.VMEM((d // LANES, tm, LANES), F32)],
        compiler_params=pltpu.CompilerParams(
            dimension_semantics=("parallel",), vmem_limit_bytes=VMEM_LIMIT_BYTES),
        name="out_proj",
    )(hg, at4, w_out, x, g, b)


def _encoder_layer(x, w, layer):
    batch, seq, d = x.shape
    h = x.reshape(batch * seq, d)
    h, hb = _ffn(h, w["ffn1_w_gate"][layer], w["ffn1_w_up"][layer], w["ffn1_w_down"][layer],
                 w["ln1_g"][layer:layer + 1], w["ln1_b"][layer:layer + 1], emit_bf16=True)
    hg, qkv = _in_proj(hb, w["w_in"][layer])
    hgrn_out = _hgrn(hg, w["hgrn_lb_fwd"], w["hgrn_lb_bwd"], w["hgrn_norm_g"][layer], batch, seq)
    attn_out = _attn(qkv, batch, seq)
    h = _out_proj(hgrn_out.reshape(batch * seq, HGRN_WIDTH), attn_out, w["w_out"][layer], h,
                  w["ln2_g"][layer:layer + 1], w["ln2_b"][layer:layer + 1])
    h, = _ffn(h, w["ffn2_w_gate"][layer], w["ffn2_w_up"][layer], w["ffn2_w_down"][layer],
              w["ln3_g"][layer:layer + 1], w["ln3_b"][layer:layer + 1], emit_bf16=False)
    return h.reshape(batch, seq, d)


_MATMUL_WEIGHTS = ("ffn1_w_gate", "ffn1_w_up", "ffn1_w_down", "w_in", "w_out",
                   "ffn2_w_gate", "ffn2_w_up", "ffn2_w_down")


def kernel(x_prompt, x_sample, ln1_g, ln1_b, ffn1_w_gate, ffn1_w_up, ffn1_w_down, ln2_g, ln2_b, w_in,
           hgrn_lb_fwd, hgrn_lb_bwd, hgrn_norm_g, w_out, ln3_g, ln3_b, ffn2_w_gate, ffn2_w_up, ffn2_w_down):
    w = dict(ln1_g=ln1_g, ln1_b=ln1_b, ffn1_w_gate=ffn1_w_gate, ffn1_w_up=ffn1_w_up, ffn1_w_down=ffn1_w_down,
             ln2_g=ln2_g, ln2_b=ln2_b, w_in=w_in, hgrn_lb_fwd=hgrn_lb_fwd, hgrn_lb_bwd=hgrn_lb_bwd,
             hgrn_norm_g=hgrn_norm_g, w_out=w_out, ln3_g=ln3_g, ln3_b=ln3_b,
             ffn2_w_gate=ffn2_w_gate, ffn2_w_up=ffn2_w_up, ffn2_w_down=ffn2_w_down)
    for name in _MATMUL_WEIGHTS:
        w[name] = w[name].astype(BF16)
    outs = []
    for x in (x_prompt, x_sample):
        for layer in range(DEPTH):
            x = _encoder_layer(x, w, layer)
        outs.append(x)
    return tuple(outs)
```

```python
import functools

import numpy as np
import jax
import jax.numpy as jnp
from jax import lax
from jax.experimental import pallas as pl
from jax.experimental.pallas import tpu as pltpu

F32 = jnp.float32
BF16 = jnp.bfloat16

D_MODEL = 2048
D_FF = 5632
DEPTH = 1
HEAD_DIM = 128
N_HEADS_HGRN = 8
N_HEADS_ATTN = 8
HGRN_WIDTH = N_HEADS_HGRN * HEAD_DIM
ATTN_WIDTH = N_HEADS_ATTN * HEAD_DIM
IN_WIDTH = 5 * HGRN_WIDTH + 3 * ATTN_WIDTH
ALPHA = (2 * DEPTH) ** 0.25
LN_EPS = 1e-5
RMS_EPS = 1e-6
NEG_INF = -1e30

CHUNK = 64
HGRN_LEVELS = (1, 2, 4, 8, 16, 32)
N_LEVELS = len(HGRN_LEVELS)

ATTN_HALF = 64
ATTN_DILATIONS = (1, 4, 16)
ATTN_CLASSES = 4
ATTN_BLOCK = 128
ATTN_PAD = 256
ATTN_SAME = ATTN_BLOCK + 2 * ATTN_PAD
ATTN_OTHER_LEAD = 32
ATTN_OTHER = ATTN_BLOCK + 2 * ATTN_OTHER_LEAD
ATTN_UNROLL = 4
LANES = 128
SUBLANES = 8

VMEM_LIMIT_BYTES = 56 * 1024 * 1024
FFN_VMEM_LIMIT_BYTES = 60 * 1024 * 1024


def _layer_norm(y, g, b):
    mu = jnp.mean(y, axis=-1, keepdims=True)
    yc = y - mu
    var = jnp.mean(yc * yc, axis=-1, keepdims=True)
    return yc * lax.rsqrt(var + LN_EPS) * g + b


def _dot_nt(a, b):
    return lax.dot_general(a, b, (((1,), (1,)), ((), ())), preferred_element_type=F32)


def _dot_tn(a, b):
    return lax.dot_general(a, b, (((0,), (0,)), ((), ())), preferred_element_type=F32)


def _ffn_kernel(x_ref, wg_ref, wu_ref, wd_ref, g_ref, b_ref, *refs, n_ff_tiles, emit_bf16):
    if emit_bf16:
        o_ref, ob_ref, xb_ref = refs
    else:
        o_ref, xb_ref = refs
    j = pl.program_id(1)

    @pl.when(j == 0)
    def _():
        xb_ref[...] = x_ref[...].astype(BF16)
        o_ref[...] = jnp.zeros_like(o_ref)

    xb = xb_ref[...]
    gate = jnp.dot(xb, wg_ref[...], preferred_element_type=F32)
    up = jnp.dot(xb, wu_ref[...], preferred_element_type=F32)
    hidden = (gate * jax.nn.sigmoid(gate) * up).astype(BF16)
    o_ref[...] += jnp.dot(hidden, wd_ref[...], preferred_element_type=F32)

    @pl.when(j == n_ff_tiles - 1)
    def _():
        y = _layer_norm(ALPHA * x_ref[...] + 0.5 * o_ref[...], g_ref[...], b_ref[...])
        o_ref[...] = y
        if emit_bf16:
            ob_ref[...] = y.astype(BF16)


def _ffn(x, wg, wu, wd, g, b, *, emit_bf16, tm=1024, tf=256):
    t, d = x.shape
    n_ff_tiles = D_FF // tf
    tok = lambda i, j: (i, 0)
    out_specs = [pl.BlockSpec((tm, d), tok)]
    out_shape = [jax.ShapeDtypeStruct((t, d), F32)]
    if emit_bf16:
        out_specs.append(pl.BlockSpec((tm, d), tok))
        out_shape.append(jax.ShapeDtypeStruct((t, d), BF16))
    return pl.pallas_call(
        functools.partial(_ffn_kernel, n_ff_tiles=n_ff_tiles, emit_bf16=emit_bf16),
        grid=(t // tm, n_ff_tiles),
        in_specs=[
            pl.BlockSpec((tm, d), tok),
            pl.BlockSpec((d, tf), lambda i, j: (0, j)),
            pl.BlockSpec((d, tf), lambda i, j: (0, j)),
            pl.BlockSpec((tf, d), lambda i, j: (j, 0)),
            pl.BlockSpec((1, d), lambda i, j: (0, 0)),
            pl.BlockSpec((1, d), lambda i, j: (0, 0)),
        ],
        out_specs=out_specs,
        out_shape=out_shape,
        scratch_shapes=[pltpu.VMEM((tm, d), BF16)],
        compiler_params=pltpu.CompilerParams(
            dimension_semantics=("parallel", "arbitrary"), vmem_limit_bytes=FFN_VMEM_LIMIT_BYTES),
        name="ffn",
    )(x, wg, wu, wd, g, b)


N_HGRN_PARTS = 5
N_ATTN_PARTS = 3
IN_TILE = 1024


def _proj_heads_kernel(x_ref, w_ref, o_ref):
    r = jnp.dot(x_ref[...], w_ref[...], preferred_element_type=F32)
    for c in range(N_HEADS_HGRN):
        o_ref[c] = r[:, c * HEAD_DIM:(c + 1) * HEAD_DIM]


def _store_lane_tiles(ref, x):
    for j in range(ref.shape[0]):
        ref[j] = x[:, j * LANES:(j + 1) * LANES]


def _proj_classes_kernel(x_ref, w_ref, o_ref, r_ref):
    _store_lane_tiles(r_ref, jnp.dot(x_ref[...], w_ref[...], preferred_element_type=F32))
    rows = o_ref.shape[0]
    for c in range(ATTN_CLASSES):
        cls_rows = pl.ds(c, rows, stride=ATTN_CLASSES)
        for j in range(r_ref.shape[0]):
            lo = c * ATTN_WIDTH + j * LANES
            o_ref[:, lo:lo + LANES] = r_ref[j, cls_rows, :].astype(BF16)


def _in_proj(xb, w_in, *, tm=1024):
    t, d = xb.shape
    params = pltpu.CompilerParams(dimension_semantics=("parallel", "arbitrary"), vmem_limit_bytes=VMEM_LIMIT_BYTES)
    hg = pl.pallas_call(
        _proj_heads_kernel,
        grid=(t // tm, N_HGRN_PARTS),
        in_specs=[pl.BlockSpec((tm, d), lambda i, j: (i, 0)), pl.BlockSpec((d, IN_TILE), lambda i, j: (0, j))],
        out_specs=pl.BlockSpec((N_HEADS_HGRN, tm, HEAD_DIM), lambda i, j: (j, i, 0)),
        out_shape=jax.ShapeDtypeStruct((N_HGRN_PARTS * N_HEADS_HGRN, t, HEAD_DIM), F32),
        compiler_params=params,
        name="in_proj_hgrn",
    )(xb, w_in)
    qkv = pl.pallas_call(
        _proj_classes_kernel,
        grid=(t // tm, N_ATTN_PARTS),
        in_specs=[pl.BlockSpec((tm, d), lambda i, j: (i, 0)),
                  pl.BlockSpec((d, IN_TILE), lambda i, j: (0, N_HGRN_PARTS + j))],
        out_specs=pl.BlockSpec((None, tm // ATTN_CLASSES, ATTN_CLASSES * ATTN_WIDTH), lambda i, j: (j, i, 0)),
        out_shape=jax.ShapeDtypeStruct((N_ATTN_PARTS, t // ATTN_CLASSES, ATTN_CLASSES * ATTN_WIDTH), BF16),
        scratch_shapes=[pltpu.VMEM((IN_TILE // LANES, tm, LANES), F32)],
        compiler_params=params,
        name="in_proj_attn",
    )(xb, w_in)
    return hg, qkv


def _hgrn_masks():
    t = np.arange(CHUNK)[:, None]
    u = np.arange(CHUNK)[None, :]
    masks = [(t == u)]
    for c in HGRN_LEVELS:
        same = (t // (2 * c)) == (u // (2 * c))
        masks.append(same & ((t % (2 * c)) >= c) & ((u % (2 * c)) < c))
    return np.stack(masks, axis=0).astype(np.float32)


def _rows(x, r, n):
    return jnp.broadcast_to(x[r:r + 1], (n, x.shape[-1]))


def _chunk_cumsum(x, reverse):
    sub = lax.broadcasted_iota(jnp.int32, x.shape, 0) % SUBLANES
    step = 1
    while step < SUBLANES:
        if reverse:
            x = x + jnp.where(sub < SUBLANES - step, pltpu.roll(x, CHUNK - step, axis=0), 0.0)
        else:
            x = x + jnp.where(sub >= step, pltpu.roll(x, step, axis=0), 0.0)
        step *= 2
    tiles = [x[j:j + SUBLANES] for j in range(0, CHUNK, SUBLANES)]
    if reverse:
        tiles = tiles[::-1]
    edge = 0 if reverse else SUBLANES - 1
    for j in range(1, len(tiles)):
        tiles[j] = tiles[j] + _rows(tiles[j - 1], edge, SUBLANES)
    if reverse:
        tiles = tiles[::-1]
    return jnp.concatenate(tiles, axis=0)


def _query_side(c, reverse):
    t = np.arange(CHUNK)
    return ((t % (2 * c)) >= c) != reverse


def _hgrn_signs(reverse):
    log2e = np.float32(np.log2(np.e))
    sign = np.stack([np.where(_query_side(c, reverse), log2e, -log2e) for c in HGRN_LEVELS])
    return np.broadcast_to(sign[:, :, None], (N_LEVELS, CHUNK, HEAD_DIM)).astype(np.float32)


def _level_boundary(g, c, reverse):
    first = c if reverse else c - 1
    if 2 * c >= SUBLANES:
        return jnp.concatenate([_rows(g, b, 2 * c) for b in range(first, CHUNK, 2 * c)], axis=0)
    sub = lax.broadcasted_iota(jnp.int32, (SUBLANES, g.shape[-1]), 0)
    return jnp.concatenate(
        [jnp.where(sub < 2 * c, _rows(g, j + first, SUBLANES), _rows(g, j + 2 * c + first, SUBLANES))
         for j in range(0, CHUNK, SUBLANES)], axis=0)


def _hgrn_chunk(q, z, v, lb, masks, signs, state_ref, reverse):
    f = lb + (1.0 - lb) * jax.nn.sigmoid(z)
    k = 1.0 - f
    g_incl = _chunk_cumsum(jnp.log(f), reverse)
    last_row = 0 if reverse else CHUNK - 1
    decay_in = jnp.exp(g_incl)
    decay_out = jnp.exp(_rows(g_incl, last_row, CHUNK) - g_incl)
    decay_all = jnp.exp(g_incl[last_row:last_row + 1])

    qb, kb, vb = q.astype(BF16), k.astype(BF16), v.astype(BF16)
    diag = _dot_nt(qb, kb) * masks[0]
    a = [diag[j:j + SUBLANES] for j in range(0, CHUNK, SUBLANES)]
    sub = lax.broadcasted_iota(jnp.int32, q.shape, 0)
    for lv, c in enumerate(HGRN_LEVELS):
        mask = masks[lv + 1]
        query_side = _query_side(c, reverse)
        if c < SUBLANES:
            rows_q = ((sub % (2 * c)) >= c) != reverse
            if c == 1:
                mixed = jnp.where(rows_q, q * f, k)
            else:
                decay = jnp.exp2((g_incl - _level_boundary(g_incl, c, reverse)) * signs[lv])
                mixed = jnp.where(rows_q, q, k) * decay
            mixed = mixed.astype(BF16)
            prod = _dot_nt(mixed, mixed) * mask
            a = [a[j] + prod[j * SUBLANES:(j + 1) * SUBLANES] for j in range(len(a))]
            continue
        decay = jnp.exp2((g_incl - _level_boundary(g_incl, c, reverse)) * signs[lv])
        mixed = jnp.concatenate([(q if query_side[r] else k)[r:r + c] for r in range(0, CHUNK, c)], axis=0) * decay
        q_rows = [r for r in range(0, CHUNK, SUBLANES) if query_side[r]]
        lhs = jnp.concatenate([mixed[r:r + SUBLANES] for r in q_rows], axis=0).astype(BF16)
        prod = _dot_nt(lhs, mixed.astype(BF16))
        for i, r in enumerate(q_rows):
            j = r // SUBLANES
            a[j] = a[j] + prod[i * SUBLANES:(i + 1) * SUBLANES] * mask[r:r + SUBLANES]
    a = jnp.concatenate(a, axis=0)
    state = state_ref[...]
    out = jnp.dot(a.astype(BF16), vb, preferred_element_type=F32)
    out = out + _dot_nt((q * decay_in).astype(BF16), state.astype(BF16))
    state_ref[...] = state * decay_all + _dot_tn(vb, (k * decay_out).astype(BF16))
    return out


def _hgrn_kernel(qf_ref, zf_ref, vf_ref, gf_ref, qb_ref, zb_ref, vb_ref, gb_ref, lbf_ref, lbb_ref, ng_ref,
                 mf_ref, mb_ref, sgf_ref, sgb_ref, o_ref, part_ref, stf_ref, stb_ref, *, n_tiles, tile):
    s = pl.program_id(2)
    n_heads = qf_ref.shape[0]
    chunks = tile // CHUNK

    @pl.when(s == 0)
    def _():
        stf_ref[...] = jnp.zeros_like(stf_ref)
        stb_ref[...] = jnp.zeros_like(stb_ref)

    def lower_bound(ref, h):
        p = ref[h]
        p = jnp.exp(p - jnp.max(p, axis=0, keepdims=True))
        return p[0:1] / jnp.sum(p, axis=0, keepdims=True)

    def scan(finish):
        def scan_step(c, carry):
            lf = pl.multiple_of(c * CHUNK, CHUNK)
            lb = pl.multiple_of((chunks - 1 - c) * CHUNK, CHUNK)
            rows_f = pl.ds(lf, CHUNK)
            rows_b = pl.ds(lb, CHUNK)
            out_f = pl.ds(pl.multiple_of(s * tile + lf, CHUNK), CHUNK)
            out_b = pl.ds(pl.multiple_of((n_tiles - 1 - s) * tile + lb, CHUNK), CHUNK)
            res = []
            for h in range(n_heads):
                res.append(_hgrn_chunk(qf_ref[h, rows_f, :], zf_ref[h, rows_f, :], vf_ref[h, rows_f, :],
                                       lower_bound(lbf_ref, h), mf_ref[...], sgf_ref[...], stf_ref.at[h], False))
                res.append(_hgrn_chunk(qb_ref[h, rows_b, :], zb_ref[h, rows_b, :], vb_ref[h, rows_b, :],
                                       lower_bound(lbb_ref, h), mb_ref[...], sgb_ref[...], stb_ref.at[h], True))
            for h in range(n_heads):
                for o, rows, out_rows, g_ref in ((res[2 * h], rows_f, out_f, gf_ref),
                                                 (res[2 * h + 1], rows_b, out_b, gb_ref)):
                    if not finish:
                        part_ref[h, out_rows, :] = o
                        continue
                    o = o + part_ref[h, out_rows, :]
                    o = o * lax.rsqrt(jnp.mean(o * o, axis=-1, keepdims=True) + RMS_EPS)
                    gate = g_ref[h, rows, :]
                    o = o * ng_ref[h] * (gate * jax.nn.sigmoid(gate))
                    o_ref[out_rows, h * HEAD_DIM:(h + 1) * HEAD_DIM] = o.astype(BF16)
            return carry

        lax.fori_loop(0, chunks, scan_step, 0, unroll=HGRN_UNROLL)

    @pl.when(s < n_tiles // 2)
    def _():
        scan(finish=False)

    @pl.when(s >= n_tiles // 2)
    def _():
        scan(finish=True)


HGRN_HEADS_PER_STEP = 4
HGRN_TILE = 512
HGRN_UNROLL = 2


def _hgrn(hg, lb_fwd, lb_bwd, norm_g, batch, seq):
    hg5 = hg.reshape(N_HGRN_PARTS, N_HEADS_HGRN, batch, seq, HEAD_DIM)
    masks = _hgrn_masks()
    masks_f = jnp.asarray(masks, F32)
    masks_b = jnp.asarray(masks[:, ::-1, ::-1], F32)
    hps, tile = HGRN_HEADS_PER_STEP, HGRN_TILE
    n_tiles = seq // tile
    assert seq % tile == 0 and n_tiles % 2 == 0 and N_HEADS_HGRN % hps == 0

    def part(p, backward):
        def index(b, hg_, s):
            return (p, hg_, b, (n_tiles - 1 - s) if backward else s, 0)
        return pl.BlockSpec((None, hps, None, tile, HEAD_DIM), index)

    def per_head(rows):
        return pl.BlockSpec((hps, rows, HEAD_DIM), lambda b, hg_, s: (hg_, 0, 0))

    def lb_per_head(lb):
        return lb.reshape(DEPTH + 1, N_HEADS_HGRN, HEAD_DIM).transpose(1, 0, 2)

    const3 = lambda b, hg_, s: (0, 0, 0)
    q, zf, zb, v, g = range(N_HGRN_PARTS)
    return pl.pallas_call(
        functools.partial(_hgrn_kernel, n_tiles=n_tiles, tile=tile),
        grid=(batch, N_HEADS_HGRN // hps, n_tiles),
        in_specs=[
            part(q, False), part(zf, False), part(v, False), part(g, False),
            part(q, True), part(zb, True), part(v, True), part(g, True),
            per_head(DEPTH + 1), per_head(DEPTH + 1), per_head(1),
            pl.BlockSpec((N_LEVELS + 1, CHUNK, CHUNK), const3),
            pl.BlockSpec((N_LEVELS + 1, CHUNK, CHUNK), const3),
            pl.BlockSpec((N_LEVELS, CHUNK, HEAD_DIM), const3),
            pl.BlockSpec((N_LEVELS, CHUNK, HEAD_DIM), const3),
        ],
        out_specs=pl.BlockSpec((None, seq, hps * HEAD_DIM), lambda b, hg_, s: (b, 0, hg_)),
        out_shape=jax.ShapeDtypeStruct((batch, seq, HGRN_WIDTH), BF16),
        scratch_shapes=[
            pltpu.VMEM((hps, seq, HEAD_DIM), F32),
            pltpu.VMEM((hps, HEAD_DIM, HEAD_DIM), F32), pltpu.VMEM((hps, HEAD_DIM, HEAD_DIM), F32),
        ],
        compiler_params=pltpu.CompilerParams(
            dimension_semantics=("parallel", "parallel", "arbitrary"), vmem_limit_bytes=VMEM_LIMIT_BYTES),
        name="hgrn",
    )(hg5, hg5, hg5, hg5, hg5, hg5, hg5, hg5,
      lb_per_head(lb_fwd), lb_per_head(lb_bwd), norm_g.reshape(N_HEADS_HGRN, 1, HEAD_DIM),
      masks_f, masks_b, jnp.asarray(_hgrn_signs(False)), jnp.asarray(_hgrn_signs(True)))


def _attn_block_types(n_rows):
    row = np.arange(ATTN_BLOCK)

    def flags(n):
        i = n * ATTN_BLOCK + row
        reach = [ATTN_HALF * d // ATTN_CLASSES for d in ATTN_DILATIONS]
        return np.stack([i < r for r in reach] + [i >= n_rows - r for r in reach]).tobytes()

    starts = [0]
    for n in range(1, n_rows // ATTN_BLOCK):
        if flags(n) != flags(n - 1):
            starts.append(n)
    return tuple(starts)


def _attn_bias(delta, i_query, slope, n_rows):
    mult = jnp.zeros(delta.shape, F32)
    for d in ATTN_DILATIONS:
        reach = ATTN_HALF * d // ATTN_CLASSES
        lo = jnp.where(i_query < reach, 0, -ATTN_HALF * d)
        hi = jnp.where(i_query >= n_rows - reach, (ATTN_HALF - 1) * d, ATTN_HALF * d)
        hit = (delta >= lo) & (delta <= hi)
        if d > 1:
            hit = hit & (lax.rem(delta, d) == 0)
        mult = mult + hit.astype(F32)
    dist = jnp.abs(delta).astype(F32)
    return jnp.where(mult > 0.0, jnp.log(jnp.maximum(mult, 1.0)) - slope * dist, NEG_INF)


def _lane_reduce(parts, combine, reduce):
    folded = {}
    for part in parts:
        for j in range(0, part.shape[-1], LANES):
            tile = part[:, j:j + LANES]
            w = tile.shape[-1]
            folded[w] = tile if w not in folded else combine(folded[w], tile)
    out = None
    for tile in folded.values():
        r = reduce(tile, axis=-1, keepdims=True)
        out = r if out is None else combine(out, r)
    return out


def _attn_kernel(slope_ref, q_ref, k0_ref, k1_ref, k2_ref, k3_ref, v0_ref, v1_ref, v2_ref, v3_ref,
                 o_ref, kpad_ref, vpad_ref, bias_same_ref, bias_other_ref, *, n_rows, type_starts):
    b, rq = pl.program_id(1), pl.program_id(2)

    @pl.when((b == 0) & (rq == 0))
    def _():
        slope = slope_ref[...][:, 0:1]
        row = lax.broadcasted_iota(jnp.int32, (ATTN_BLOCK, ATTN_SAME), 0)
        col = lax.broadcasted_iota(jnp.int32, (ATTN_BLOCK, ATTN_SAME), 1)
        for ty, first_block in enumerate(type_starts):
            bias_same_ref[ty] = _attn_bias((col - ATTN_PAD - row) * ATTN_CLASSES,
                                           first_block * ATTN_BLOCK + row, slope, n_rows)
        row = lax.broadcasted_iota(jnp.int32, (ATTN_BLOCK, ATTN_OTHER), 0)
        col = lax.broadcasted_iota(jnp.int32, (ATTN_BLOCK, ATTN_OTHER), 1)
        for query_class in range(ATTN_CLASSES):
            for m in range(1, ATTN_CLASSES):
                delta = (col - ATTN_OTHER_LEAD - row) * ATTN_CLASSES + ((query_class + m) % ATTN_CLASSES - query_class)
                for ty, first_block in enumerate(type_starts):
                    bias_other_ref[query_class, ty, m - 1] = _attn_bias(
                        delta, first_block * ATTN_BLOCK + row, slope, n_rows)

    @pl.when(rq == 0)
    def _():
        zeros = jnp.zeros((ATTN_PAD, HEAD_DIM), BF16)
        for c, (k_ref, v_ref) in enumerate(((k0_ref, v0_ref), (k1_ref, v1_ref), (k2_ref, v2_ref), (k3_ref, v3_ref))):
            for pad_ref, src_ref in ((kpad_ref, k_ref), (vpad_ref, v_ref)):
                pad_ref[c, 0:ATTN_PAD, :] = zeros
                pad_ref[c, ATTN_PAD:ATTN_PAD + n_rows, :] = src_ref[...]
                pad_ref[c, ATTN_PAD + n_rows:ATTN_PAD + n_rows + ATTN_PAD, :] = zeros

    scale = HEAD_DIM ** -0.5

    def block(n, carry):
        i0 = pl.multiple_of(n * ATTN_BLOCK, ATTN_BLOCK)
        ty = jnp.int32(0)
        for first_block in type_starts[1:]:
            ty = ty + jnp.asarray(n >= first_block, jnp.int32)
        qb = q_ref[pl.ds(i0, ATTN_BLOCK), :]
        scores, values = [], []
        s = _dot_nt(qb, kpad_ref[rq, pl.ds(i0, ATTN_SAME), :]) * scale
        scores.append(s + bias_same_ref[ty])
        values.append(vpad_ref[rq, pl.ds(i0, ATTN_SAME), :])
        start = pl.multiple_of(i0 + (ATTN_PAD - ATTN_OTHER_LEAD), ATTN_OTHER_LEAD)
        for m in range(1, ATTN_CLASSES):
            rk = lax.rem(rq + m, ATTN_CLASSES)
            s = _dot_nt(qb, kpad_ref[rk, pl.ds(start, ATTN_OTHER), :]) * scale
            scores.append(s + bias_other_ref[rq, ty, m - 1])
            values.append(vpad_ref[rk, pl.ds(start, ATTN_OTHER), :])
        mx = _lane_reduce(scores, jnp.maximum, jnp.max)
        probs = [jnp.exp(s - mx) for s in scores]
        den = _lane_reduce(probs, jnp.add, jnp.sum)
        acc = jnp.zeros((ATTN_BLOCK, HEAD_DIM), F32)
        for p, v in zip(probs, values):
            acc = acc + jnp.dot(p.astype(BF16), v, preferred_element_type=F32)
        o_ref[pl.ds(i0, ATTN_BLOCK), :] = (acc / den).astype(BF16)
        return carry

    lax.fori_loop(0, n_rows // ATTN_BLOCK, block, 0, unroll=ATTN_UNROLL)


def _attn(qkv, batch, seq):
    n_rows = seq // ATTN_CLASSES
    assert seq % (ATTN_HALF * max(ATTN_DILATIONS)) == 0 and seq >= 2 * ATTN_HALF * max(ATTN_DILATIONS)
    assert n_rows % (ATTN_BLOCK * ATTN_UNROLL) == 0
    type_starts = _attn_block_types(n_rows)
    n_types = len(type_starts)
    qkv = qkv.reshape(N_ATTN_PARTS, batch, n_rows, ATTN_CLASSES * ATTN_WIDTH)
    slopes = 2.0 ** (-8.0 * (jnp.arange(N_HEADS_ATTN, dtype=F32) + 1.0) / N_HEADS_ATTN)
    slopes = jnp.broadcast_to(slopes[:, None, None], (N_HEADS_ATTN, 1, HEAD_DIM))

    def cls(part, c):
        return pl.BlockSpec((None, None, n_rows, HEAD_DIM), lambda h, b, r: (part, b, 0, c * N_HEADS_ATTN + h))

    padded = n_rows + 2 * ATTN_PAD
    out = pl.pallas_call(
        functools.partial(_attn_kernel, n_rows=n_rows, type_starts=type_starts),
        grid=(N_HEADS_ATTN, batch, ATTN_CLASSES),
        in_specs=[pl.BlockSpec((None, 1, HEAD_DIM), lambda h, b, r: (h, 0, 0)),
                  pl.BlockSpec((None, None, n_rows, HEAD_DIM), lambda h, b, r: (0, b, 0, r * N_HEADS_ATTN + h))]
                 + [cls(1, c) for c in range(ATTN_CLASSES)] + [cls(2, c) for c in range(ATTN_CLASSES)],
        out_specs=pl.BlockSpec((None, n_rows, HEAD_DIM), lambda h, b, r: (b, 0, r * N_HEADS_ATTN + h)),
        out_shape=jax.ShapeDtypeStruct((batch, n_rows, ATTN_CLASSES * ATTN_WIDTH), BF16),
        scratch_shapes=[
            pltpu.VMEM((ATTN_CLASSES, padded, HEAD_DIM), BF16),
            pltpu.VMEM((ATTN_CLASSES, padded, HEAD_DIM), BF16),
            pltpu.VMEM((n_types, ATTN_BLOCK, ATTN_SAME), F32),
            pltpu.VMEM((ATTN_CLASSES, n_types, ATTN_CLASSES - 1, ATTN_BLOCK, ATTN_OTHER), F32),
        ],
        compiler_params=pltpu.CompilerParams(
            dimension_semantics=("arbitrary", "arbitrary", "arbitrary"), vmem_limit_bytes=VMEM_LIMIT_BYTES),
        name="attn",
    )(slopes, *([qkv] * (1 + 2 * ATTN_CLASSES)))
    return out.reshape(batch * n_rows, ATTN_CLASSES * ATTN_WIDTH)


def _out_proj_kernel(hg_ref, at_ref, w_ref, x_ref, g_ref, b_ref, o_ref, y_ref):
    _store_lane_tiles(y_ref, ALPHA * x_ref[...]
                      + jnp.dot(hg_ref[...], w_ref[0:HGRN_WIDTH, :], preferred_element_type=F32))
    rows = at_ref.shape[0]
    n_tiles = y_ref.shape[0]
    for c in range(ATTN_CLASSES):
        mix = jnp.dot(at_ref[:, c * ATTN_WIDTH:(c + 1) * ATTN_WIDTH], w_ref[HGRN_WIDTH:HGRN_WIDTH + ATTN_WIDTH, :],
                      preferred_element_type=F32)
        cls_rows = pl.ds(c, rows, stride=ATTN_CLASSES)
        y = jnp.concatenate([y_ref[j, cls_rows, :] for j in range(n_tiles)], axis=-1) + mix
        y = _layer_norm(y, g_ref[...], b_ref[...])
        for j in range(n_tiles):
            y_ref[j, cls_rows, :] = y[:, j * LANES:(j + 1) * LANES]
    o_ref[...] = jnp.concatenate([y_ref[j] for j in range(n_tiles)], axis=-1)


def _out_proj(hg, at4, w_out, x, g, b, *, tm=512):
    t, d = x.shape
    tok = lambda i: (i, 0)
    const = lambda i: (0, 0)
    return pl.pallas_call(
        _out_proj_kernel,
        grid=(t // tm,),
        in_specs=[
            pl.BlockSpec((tm, HGRN_WIDTH), tok),
            pl.BlockSpec((tm // ATTN_CLASSES, ATTN_CLASSES * ATTN_WIDTH), tok),
            pl.BlockSpec((HGRN_WIDTH + ATTN_WIDTH, d), const),
            pl.BlockSpec((tm, d), tok),
            pl.BlockSpec((1, d), const),
            pl.BlockSpec((1, d), const),
        ],
        out_specs=pl.BlockSpec((tm, d), tok),
        out_shape=jax.ShapeDtypeStruct((t, d), F32),
        scratch_shapes=[pltpu.VMEM((d // LANES, tm, LANES), F32)],
        compiler_params=pltpu.CompilerParams(
            dimension_semantics=("parallel",), vmem_limit_bytes=VMEM_LIMIT_BYTES),
        name="out_proj",
    )(hg, at4, w_out, x, g, b)


def _encoder_layer(x, w, layer):
    batch, seq, d = x.shape
    h = x.reshape(batch * seq, d)
    h, hb = _ffn(h, w["ffn1_w_gate"][layer], w["ffn1_w_up"][layer], w["ffn1_w_down"][layer],
                 w["ln1_g"][layer:layer + 1], w["ln1_b"][layer:layer + 1], emit_bf16=True)
    hg, qkv = _in_proj(hb, w["w_in"][layer])
    hgrn_out = _hgrn(hg, w["hgrn_lb_fwd"], w["hgrn_lb_bwd"], w["hgrn_norm_g"][layer], batch, seq)
    attn_out = _attn(qkv, batch, seq)
    h = _out_proj(hgrn_out.reshape(batch * seq, HGRN_WIDTH), attn_out, w["w_out"][layer], h,
                  w["ln2_g"][layer:layer + 1], w["ln2_b"][layer:layer + 1])
    h, = _ffn(h, w["ffn2_w_gate"][layer], w["ffn2_w_up"][layer], w["ffn2_w_down"][layer],
              w["ln3_g"][layer:layer + 1], w["ln3_b"][layer:layer + 1], emit_bf16=False)
    return h.reshape(batch, seq, d)


_MATMUL_WEIGHTS = ("ffn1_w_gate", "ffn1_w_up", "ffn1_w_down", "w_in", "w_out",
                   "ffn2_w_gate", "ffn2_w_up", "ffn2_w_down")


def kernel(x_prompt, x_sample, ln1_g, ln1_b, ffn1_w_gate, ffn1_w_up, ffn1_w_down, ln2_g, ln2_b, w_in,
           hgrn_lb_fwd, hgrn_lb_bwd, hgrn_norm_g, w_out, ln3_g, ln3_b, ffn2_w_gate, ffn2_w_up, ffn2_w_down):
    w = dict(ln1_g=ln1_g, ln1_b=ln1_b, ffn1_w_gate=ffn1_w_gate, ffn1_w_up=ffn1_w_up, ffn1_w_down=ffn1_w_down,
             ln2_g=ln2_g, ln2_b=ln2_b, w_in=w_in, hgrn_lb_fwd=hgrn_lb_fwd, hgrn_lb_bwd=hgrn_lb_bwd,
             hgrn_norm_g=hgrn_norm_g, w_out=w_out, ln3_g=ln3_g, ln3_b=ln3_b,
             ffn2_w_gate=ffn2_w_gate, ffn2_w_up=ffn2_w_up, ffn2_w_down=ffn2_w_down)
    for name in _MATMUL_WEIGHTS:
        w[name] = w[name].astype(BF16)
    outs = []
    for x in (x_prompt, x_sample):
        for layer in range(DEPTH):
            x = _encoder_layer(x, w, layer)
        outs.append(x)
    return tuple(outs)
```

```python
import functools

import numpy as np
import jax
import jax.numpy as jnp
from jax import lax
from jax.experimental import pallas as pl
from jax.experimental.pallas import tpu as pltpu

F32 = jnp.float32
BF16 = jnp.bfloat16

D_MODEL = 2048
D_FF = 5632
DEPTH = 1
HEAD_DIM = 128
N_HEADS_HGRN = 8
N_HEADS_ATTN = 8
HGRN_WIDTH = N_HEADS_HGRN * HEAD_DIM
ATTN_WIDTH = N_HEADS_ATTN * HEAD_DIM
IN_WIDTH = 5 * HGRN_WIDTH + 3 * ATTN_WIDTH
ALPHA = (2 * DEPTH) ** 0.25
LN_EPS = 1e-5
RMS_EPS = 1e-6
NEG_INF = -1e30

CHUNK = 64
HGRN_LEVELS = (1, 2, 4, 8, 16, 32)
N_LEVELS = len(HGRN_LEVELS)

ATTN_HALF = 64
ATTN_DILATIONS = (1, 4, 16)
ATTN_CLASSES = 4
ATTN_BLOCK = 128
ATTN_PAD = 256
ATTN_SAME = ATTN_BLOCK + 2 * ATTN_PAD
ATTN_OTHER_LEAD = 32
ATTN_OTHER = ATTN_BLOCK + 2 * ATTN_OTHER_LEAD
ATTN_UNROLL = 4
LANES = 128
SUBLANES = 8

VMEM_LIMIT_BYTES = 56 * 1024 * 1024
FFN_VMEM_LIMIT_BYTES = 60 * 1024 * 1024


def _layer_norm(y, g, b):
    mu = jnp.mean(y, axis=-1, keepdims=True)
    yc = y - mu
    var = jnp.mean(yc * yc, axis=-1, keepdims=True)
    return yc * lax.rsqrt(var + LN_EPS) * g + b


def _dot_nt(a, b):
    return lax.dot_general(a, b, (((1,), (1,)), ((), ())), preferred_element_type=F32)


def _dot_tn(a, b):
    return lax.dot_general(a, b, (((0,), (0,)), ((), ())), preferred_element_type=F32)


def _ffn_kernel(x_ref, wg_ref, wu_ref, wd_ref, g_ref, b_ref, *refs, n_ff_tiles, emit_bf16):
    if emit_bf16:
        o_ref, ob_ref, xb_ref = refs
    else:
        o_ref, xb_ref = refs
    j = pl.program_id(1)

    @pl.when(j == 0)
    def _():
        xb_ref[...] = x_ref[...].astype(BF16)
        o_ref[...] = jnp.zeros_like(o_ref)

    xb = xb_ref[...]
    gate = jnp.dot(xb, wg_ref[...], preferred_element_type=F32)
    up = jnp.dot(xb, wu_ref[...], preferred_element_type=F32)
    hidden = (gate * jax.nn.sigmoid(gate) * up).astype(BF16)
    o_ref[...] += jnp.dot(hidden, wd_ref[...], preferred_element_type=F32)

    @pl.when(j == n_ff_tiles - 1)
    def _():
        y = _layer_norm(ALPHA * x_ref[...] + 0.5 * o_ref[...], g_ref[...], b_ref[...])
        o_ref[...] = y
        if emit_bf16:
            ob_ref[...] = y.astype(BF16)


def _ffn(x, wg, wu, wd, g, b, *, emit_bf16, tm=1024, tf=256):
    t, d = x.shape
    n_ff_tiles = D_FF // tf
    tok = lambda i, j: (i, 0)
    out_specs = [pl.BlockSpec((tm, d), tok)]
    out_shape = [jax.ShapeDtypeStruct((t, d), F32)]
    if emit_bf16:
        out_specs.append(pl.BlockSpec((tm, d), tok))
        out_shape.append(jax.ShapeDtypeStruct((t, d), BF16))
    return pl.pallas_call(
        functools.partial(_ffn_kernel, n_ff_tiles=n_ff_tiles, emit_bf16=emit_bf16),
        grid=(t // tm, n_ff_tiles),
        in_specs=[
            pl.BlockSpec((tm, d), tok),
            pl.BlockSpec((d, tf), lambda i, j: (0, j)),
            pl.BlockSpec((d, tf), lambda i, j: (0, j)),
            pl.BlockSpec((tf, d), lambda i, j: (j, 0)),
            pl.BlockSpec((1, d), lambda i, j: (0, 0)),
            pl.BlockSpec((1, d), lambda i, j: (0, 0)),
        ],
        out_specs=out_specs,
        out_shape=out_shape,
        scratch_shapes=[pltpu.VMEM((tm, d), BF16)],
        compiler_params=pltpu.CompilerParams(
            dimension_semantics=("parallel", "arbitrary"), vmem_limit_bytes=FFN_VMEM_LIMIT_BYTES),
        name="ffn",
    )(x, wg, wu, wd, g, b)


N_HGRN_PARTS = 5
N_ATTN_PARTS = 3
ATTN_SLABS = ATTN_CLASSES * N_HEADS_ATTN
IN_TILE = 1024


def _proj_heads_kernel(x_ref, w_ref, o_ref):
    r = jnp.dot(x_ref[...], w_ref[...], preferred_element_type=F32)
    for c in range(N_HEADS_HGRN):
        o_ref[c] = r[:, c * HEAD_DIM:(c + 1) * HEAD_DIM]


def _store_lane_tiles(ref, x):
    for j in range(ref.shape[0]):
        ref[j] = x[:, j * LANES:(j + 1) * LANES]


def _proj_classes_kernel(x_ref, w_ref, o_ref, r_ref):
    _store_lane_tiles(r_ref, jnp.dot(x_ref[...], w_ref[...], preferred_element_type=F32))
    rows = o_ref.shape[1]
    for c in range(ATTN_CLASSES):
        cls_rows = pl.ds(c, rows, stride=ATTN_CLASSES)
        for h in range(N_HEADS_ATTN):
            o_ref[c * N_HEADS_ATTN + h] = r_ref[h, cls_rows, :].astype(BF16)


def _in_proj(xb, w_in, *, tm=1024):
    t, d = xb.shape
    params = pltpu.CompilerParams(dimension_semantics=("parallel", "arbitrary"), vmem_limit_bytes=VMEM_LIMIT_BYTES)
    hg = pl.pallas_call(
        _proj_heads_kernel,
        grid=(t // tm, N_HGRN_PARTS),
        in_specs=[pl.BlockSpec((tm, d), lambda i, j: (i, 0)), pl.BlockSpec((d, IN_TILE), lambda i, j: (0, j))],
        out_specs=pl.BlockSpec((N_HEADS_HGRN, tm, HEAD_DIM), lambda i, j: (j, i, 0)),
        out_shape=jax.ShapeDtypeStruct((N_HGRN_PARTS * N_HEADS_HGRN, t, HEAD_DIM), F32),
        compiler_params=params,
        name="in_proj_hgrn",
    )(xb, w_in)
    qkv = pl.pallas_call(
        _proj_classes_kernel,
        grid=(t // tm, N_ATTN_PARTS),
        in_specs=[pl.BlockSpec((tm, d), lambda i, j: (i, 0)),
                  pl.BlockSpec((d, IN_TILE), lambda i, j: (0, N_HGRN_PARTS + j))],
        out_specs=pl.BlockSpec((None, ATTN_SLABS, tm // ATTN_CLASSES, HEAD_DIM), lambda i, j: (j, 0, i, 0)),
        out_shape=jax.ShapeDtypeStruct((N_ATTN_PARTS, ATTN_SLABS, t // ATTN_CLASSES, HEAD_DIM), BF16),
        scratch_shapes=[pltpu.VMEM((IN_TILE // LANES, tm, LANES), F32)],
        compiler_params=params,
        name="in_proj_attn",
    )(xb, w_in)
    return hg, qkv


def _hgrn_masks():
    t = np.arange(CHUNK)[:, None]
    u = np.arange(CHUNK)[None, :]
    masks = [(t == u)]
    for c in HGRN_LEVELS:
        same = (t // (2 * c)) == (u // (2 * c))
        masks.append(same & ((t % (2 * c)) >= c) & ((u % (2 * c)) < c))
    return np.stack(masks, axis=0).astype(np.float32)


def _rows(x, r, n):
    return jnp.broadcast_to(x[r:r + 1], (n, x.shape[-1]))


def _chunk_cumsum(x, reverse):
    sub = lax.broadcasted_iota(jnp.int32, x.shape, 0) % SUBLANES
    step = 1
    while step < SUBLANES:
        if reverse:
            x = x + jnp.where(sub < SUBLANES - step, pltpu.roll(x, CHUNK - step, axis=0), 0.0)
        else:
            x = x + jnp.where(sub >= step, pltpu.roll(x, step, axis=0), 0.0)
        step *= 2
    tiles = [x[j:j + SUBLANES] for j in range(0, CHUNK, SUBLANES)]
    if reverse:
        tiles = tiles[::-1]
    edge = 0 if reverse else SUBLANES - 1
    for j in range(1, len(tiles)):
        tiles[j] = tiles[j] + _rows(tiles[j - 1], edge, SUBLANES)
    if reverse:
        tiles = tiles[::-1]
    return jnp.concatenate(tiles, axis=0)


def _query_side(c, reverse):
    t = np.arange(CHUNK)
    return ((t % (2 * c)) >= c) != reverse


def _hgrn_signs(reverse):
    log2e = np.float32(np.log2(np.e))
    sign = np.stack([np.where(_query_side(c, reverse), log2e, -log2e) for c in HGRN_LEVELS])
    return np.broadcast_to(sign[:, :, None], (N_LEVELS, CHUNK, HEAD_DIM)).astype(np.float32)


def _level_boundary(g, c, reverse):
    first = c if reverse else c - 1
    if 2 * c >= SUBLANES:
        return jnp.concatenate([_rows(g, b, 2 * c) for b in range(first, CHUNK, 2 * c)], axis=0)
    sub = lax.broadcasted_iota(jnp.int32, (SUBLANES, g.shape[-1]), 0)
    return jnp.concatenate(
        [jnp.where(sub < 2 * c, _rows(g, j + first, SUBLANES), _rows(g, j + 2 * c + first, SUBLANES))
         for j in range(0, CHUNK, SUBLANES)], axis=0)


def _hgrn_chunk(q, z, v, lb, masks, signs, state_ref, reverse):
    f = lb + (1.0 - lb) * jax.nn.sigmoid(z)
    k = 1.0 - f
    g_incl = _chunk_cumsum(jnp.log(f), reverse)
    last_row = 0 if reverse else CHUNK - 1
    decay_in = jnp.exp(g_incl)
    decay_out = jnp.exp(_rows(g_incl, last_row, CHUNK) - g_incl)
    decay_all = jnp.exp(g_incl[last_row:last_row + 1])

    qb, kb, vb = q.astype(BF16), k.astype(BF16), v.astype(BF16)
    diag = _dot_nt(qb, kb) * masks[0]
    a = [diag[j:j + SUBLANES] for j in range(0, CHUNK, SUBLANES)]
    sub = lax.broadcasted_iota(jnp.int32, q.shape, 0)
    for lv, c in enumerate(HGRN_LEVELS):
        mask = masks[lv + 1]
        query_side = _query_side(c, reverse)
        if c < SUBLANES:
            rows_q = ((sub % (2 * c)) >= c) != reverse
            if c == 1:
                mixed = jnp.where(rows_q, q * f, k)
            else:
                decay = jnp.exp2((g_incl - _level_boundary(g_incl, c, reverse)) * signs[lv])
                mixed = jnp.where(rows_q, q, k) * decay
            mixed = mixed.astype(BF16)
            prod = _dot_nt(mixed, mixed) * mask
            a = [a[j] + prod[j * SUBLANES:(j + 1) * SUBLANES] for j in range(len(a))]
            continue
        decay = jnp.exp2((g_incl - _level_boundary(g_incl, c, reverse)) * signs[lv])
        mixed = jnp.concatenate([(q if query_side[r] else k)[r:r + c] for r in range(0, CHUNK, c)], axis=0) * decay
        q_rows = [r for r in range(0, CHUNK, SUBLANES) if query_side[r]]
        lhs = jnp.concatenate([mixed[r:r + SUBLANES] for r in q_rows], axis=0).astype(BF16)
        prod = _dot_nt(lhs, mixed.astype(BF16))
        for i, r in enumerate(q_rows):
            j = r // SUBLANES
            a[j] = a[j] + prod[i * SUBLANES:(i + 1) * SUBLANES] * mask[r:r + SUBLANES]
    a = jnp.concatenate(a, axis=0)
    state = state_ref[...]
    out = jnp.dot(a.astype(BF16), vb, preferred_element_type=F32)
    out = out + _dot_nt((q * decay_in).astype(BF16), state.astype(BF16))
    state_ref[...] = state * decay_all + _dot_tn(vb, (k * decay_out).astype(BF16))
    return out


def _hgrn_kernel(qf_ref, zf_ref, vf_ref, gf_ref, qb_ref, zb_ref, vb_ref, gb_ref, lbf_ref, lbb_ref, ng_ref,
                 mf_ref, mb_ref, sgf_ref, sgb_ref, o_ref, part_ref, stf_ref, stb_ref, *, n_tiles, tile):
    s = pl.program_id(2)
    n_heads = qf_ref.shape[0]
    chunks = tile // CHUNK

    @pl.when(s == 0)
    def _():
        stf_ref[...] = jnp.zeros_like(stf_ref)
        stb_ref[...] = jnp.zeros_like(stb_ref)

    def lower_bound(ref, h):
        p = ref[h]
        p = jnp.exp(p - jnp.max(p, axis=0, keepdims=True))
        return p[0:1] / jnp.sum(p, axis=0, keepdims=True)

    def scan(finish):
        def scan_step(c, carry):
            lf = pl.multiple_of(c * CHUNK, CHUNK)
            lb = pl.multiple_of((chunks - 1 - c) * CHUNK, CHUNK)
            rows_f = pl.ds(lf, CHUNK)
            rows_b = pl.ds(lb, CHUNK)
            out_f = pl.ds(pl.multiple_of(s * tile + lf, CHUNK), CHUNK)
            out_b = pl.ds(pl.multiple_of((n_tiles - 1 - s) * tile + lb, CHUNK), CHUNK)
            res = []
            for h in range(n_heads):
                res.append(_hgrn_chunk(qf_ref[h, rows_f, :], zf_ref[h, rows_f, :], vf_ref[h, rows_f, :],
                                       lower_bound(lbf_ref, h), mf_ref[...], sgf_ref[...], stf_ref.at[h], False))
                res.append(_hgrn_chunk(qb_ref[h, rows_b, :], zb_ref[h, rows_b, :], vb_ref[h, rows_b, :],
                                       lower_bound(lbb_ref, h), mb_ref[...], sgb_ref[...], stb_ref.at[h], True))
            for h in range(n_heads):
                for o, rows, out_rows, g_ref in ((res[2 * h], rows_f, out_f, gf_ref),
                                                 (res[2 * h + 1], rows_b, out_b, gb_ref)):
                    if not finish:
                        part_ref[h, out_rows, :] = o
                        continue
                    o = o + part_ref[h, out_rows, :]
                    o = o * lax.rsqrt(jnp.mean(o * o, axis=-1, keepdims=True) + RMS_EPS)
                    gate = g_ref[h, rows, :]
                    o = o * ng_ref[h] * (gate * jax.nn.sigmoid(gate))
                    o_ref[out_rows, h * HEAD_DIM:(h + 1) * HEAD_DIM] = o.astype(BF16)
            return carry

        lax.fori_loop(0, chunks, scan_step, 0, unroll=HGRN_UNROLL)

    @pl.when(s < n_tiles // 2)
    def _():
        scan(finish=False)

    @pl.when(s >= n_tiles // 2)
    def _():
        scan(finish=True)


HGRN_HEADS_PER_STEP = 4
HGRN_TILE = 512
HGRN_UNROLL = 4


def _hgrn(hg, lb_fwd, lb_bwd, norm_g, batch, seq):
    hg5 = hg.reshape(N_HGRN_PARTS, N_HEADS_HGRN, batch, seq, HEAD_DIM)
    masks = _hgrn_masks()
    masks_f = jnp.asarray(masks, F32)
    masks_b = jnp.asarray(masks[:, ::-1, ::-1], F32)
    hps, tile = HGRN_HEADS_PER_STEP, HGRN_TILE
    n_tiles = seq // tile
    assert seq % tile == 0 and n_tiles % 2 == 0 and N_HEADS_HGRN % hps == 0

    def part(p, backward):
        def index(b, hg_, s):
            return (p, hg_, b, (n_tiles - 1 - s) if backward else s, 0)
        return pl.BlockSpec((None, hps, None, tile, HEAD_DIM), index)

    def per_head(rows):
        return pl.BlockSpec((hps, rows, HEAD_DIM), lambda b, hg_, s: (hg_, 0, 0))

    def lb_per_head(lb):
        return lb.reshape(DEPTH + 1, N_HEADS_HGRN, HEAD_DIM).transpose(1, 0, 2)

    const3 = lambda b, hg_, s: (0, 0, 0)
    q, zf, zb, v, g = range(N_HGRN_PARTS)
    return pl.pallas_call(
        functools.partial(_hgrn_kernel, n_tiles=n_tiles, tile=tile),
        grid=(batch, N_HEADS_HGRN // hps, n_tiles),
        in_specs=[
            part(q, False), part(zf, False), part(v, False), part(g, False),
            part(q, True), part(zb, True), part(v, True), part(g, True),
            per_head(DEPTH + 1), per_head(DEPTH + 1), per_head(1),
            pl.BlockSpec((N_LEVELS + 1, CHUNK, CHUNK), const3),
            pl.BlockSpec((N_LEVELS + 1, CHUNK, CHUNK), const3),
            pl.BlockSpec((N_LEVELS, CHUNK, HEAD_DIM), const3),
            pl.BlockSpec((N_LEVELS, CHUNK, HEAD_DIM), const3),
        ],
        out_specs=pl.BlockSpec((None, seq, hps * HEAD_DIM), lambda b, hg_, s: (b, 0, hg_)),
        out_shape=jax.ShapeDtypeStruct((batch, seq, HGRN_WIDTH), BF16),
        scratch_shapes=[
            pltpu.VMEM((hps, seq, HEAD_DIM), F32),
            pltpu.VMEM((hps, HEAD_DIM, HEAD_DIM), F32), pltpu.VMEM((hps, HEAD_DIM, HEAD_DIM), F32),
        ],
        compiler_params=pltpu.CompilerParams(
            dimension_semantics=("parallel", "parallel", "arbitrary"), vmem_limit_bytes=VMEM_LIMIT_BYTES),
        name="hgrn",
    )(hg5, hg5, hg5, hg5, hg5, hg5, hg5, hg5,
      lb_per_head(lb_fwd), lb_per_head(lb_bwd), norm_g.reshape(N_HEADS_HGRN, 1, HEAD_DIM),
      masks_f, masks_b, jnp.asarray(_hgrn_signs(False)), jnp.asarray(_hgrn_signs(True)))


def _attn_block_types(n_rows):
    row = np.arange(ATTN_BLOCK)

    def flags(n):
        i = n * ATTN_BLOCK + row
        reach = [ATTN_HALF * d // ATTN_CLASSES for d in ATTN_DILATIONS]
        return np.stack([i < r for r in reach] + [i >= n_rows - r for r in reach]).tobytes()

    starts = [0]
    for n in range(1, n_rows // ATTN_BLOCK):
        if flags(n) != flags(n - 1):
            starts.append(n)
    return tuple(starts)


def _attn_bias(delta, i_query, slope, n_rows):
    mult = jnp.zeros(delta.shape, F32)
    for d in ATTN_DILATIONS:
        reach = ATTN_HALF * d // ATTN_CLASSES
        lo = jnp.where(i_query < reach, 0, -ATTN_HALF * d)
        hi = jnp.where(i_query >= n_rows - reach, (ATTN_HALF - 1) * d, ATTN_HALF * d)
        hit = (delta >= lo) & (delta <= hi)
        if d > 1:
            hit = hit & (lax.rem(delta, d) == 0)
        mult = mult + hit.astype(F32)
    dist = jnp.abs(delta).astype(F32)
    return jnp.where(mult > 0.0, jnp.log(jnp.maximum(mult, 1.0)) - slope * dist, NEG_INF)


def _lane_reduce(parts, combine, reduce):
    folded = {}
    for part in parts:
        for j in range(0, part.shape[-1], LANES):
            tile = part[:, j:j + LANES]
            w = tile.shape[-1]
            folded[w] = tile if w not in folded else combine(folded[w], tile)
    out = None
    for tile in folded.values():
        r = reduce(tile, axis=-1, keepdims=True)
        out = r if out is None else combine(out, r)
    return out


def _attn_kernel(slope_ref, q_ref, k0_ref, k1_ref, k2_ref, k3_ref, v0_ref, v1_ref, v2_ref, v3_ref,
                 o_ref, kpad_ref, vpad_ref, bias_same_ref, bias_other_ref, *, n_rows, type_starts):
    b, rq = pl.program_id(1), pl.program_id(2)

    @pl.when((b == 0) & (rq == 0))
    def _():
        slope = slope_ref[...][:, 0:1]
        row = lax.broadcasted_iota(jnp.int32, (ATTN_BLOCK, ATTN_SAME), 0)
        col = lax.broadcasted_iota(jnp.int32, (ATTN_BLOCK, ATTN_SAME), 1)
        for ty, first_block in enumerate(type_starts):
            bias_same_ref[ty] = _attn_bias((col - ATTN_PAD - row) * ATTN_CLASSES,
                                           first_block * ATTN_BLOCK + row, slope, n_rows)
        row = lax.broadcasted_iota(jnp.int32, (ATTN_BLOCK, ATTN_OTHER), 0)
        col = lax.broadcasted_iota(jnp.int32, (ATTN_BLOCK, ATTN_OTHER), 1)
        for query_class in range(ATTN_CLASSES):
            for m in range(1, ATTN_CLASSES):
                delta = (col - ATTN_OTHER_LEAD - row) * ATTN_CLASSES + ((query_class + m) % ATTN_CLASSES - query_class)
                for ty, first_block in enumerate(type_starts):
                    bias_other_ref[query_class, ty, m - 1] = _attn_bias(
                        delta, first_block * ATTN_BLOCK + row, slope, n_rows)

    @pl.when(rq == 0)
    def _():
        zeros = jnp.zeros((ATTN_PAD, HEAD_DIM), BF16)
        for c, (k_ref, v_ref) in enumerate(((k0_ref, v0_ref), (k1_ref, v1_ref), (k2_ref, v2_ref), (k3_ref, v3_ref))):
            for pad_ref, src_ref in ((kpad_ref, k_ref), (vpad_ref, v_ref)):
                pad_ref[c, 0:ATTN_PAD, :] = zeros
                pad_ref[c, ATTN_PAD:ATTN_PAD + n_rows, :] = src_ref[...]
                pad_ref[c, ATTN_PAD + n_rows:ATTN_PAD + n_rows + ATTN_PAD, :] = zeros

    scale = HEAD_DIM ** -0.5

    def block(n, carry):
        i0 = pl.multiple_of(n * ATTN_BLOCK, ATTN_BLOCK)
        ty = jnp.int32(0)
        for first_block in type_starts[1:]:
            ty = ty + jnp.asarray(n >= first_block, jnp.int32)
        qb = q_ref[pl.ds(i0, ATTN_BLOCK), :]
        scores, values = [], []
        s = _dot_nt(qb, kpad_ref[rq, pl.ds(i0, ATTN_SAME), :]) * scale
        scores.append(s + bias_same_ref[ty])
        values.append(vpad_ref[rq, pl.ds(i0, ATTN_SAME), :])
        start = pl.multiple_of(i0 + (ATTN_PAD - ATTN_OTHER_LEAD), ATTN_OTHER_LEAD)
        for m in range(1, ATTN_CLASSES):
            rk = lax.rem(rq + m, ATTN_CLASSES)
            s = _dot_nt(qb, kpad_ref[rk, pl.ds(start, ATTN_OTHER), :]) * scale
            scores.append(s + bias_other_ref[rq, ty, m - 1])
            values.append(vpad_ref[rk, pl.ds(start, ATTN_OTHER), :])
        mx = _lane_reduce(scores, jnp.maximum, jnp.max)
        probs = [jnp.exp(s - mx) for s in scores]
        den = _lane_reduce(probs, jnp.add, jnp.sum)
        acc = jnp.zeros((ATTN_BLOCK, HEAD_DIM), F32)
        for p, v in zip(probs, values):
            acc = acc + jnp.dot(p.astype(BF16), v, preferred_element_type=F32)
        o_ref[pl.ds(i0, ATTN_BLOCK), :] = (acc / den).astype(BF16)
        return carry

    lax.fori_loop(0, n_rows // ATTN_BLOCK, block, 0, unroll=ATTN_UNROLL)


def _attn(qkv, batch, seq):
    n_rows = seq // ATTN_CLASSES
    assert seq % (ATTN_HALF * max(ATTN_DILATIONS)) == 0 and seq >= 2 * ATTN_HALF * max(ATTN_DILATIONS)
    assert n_rows % (ATTN_BLOCK * ATTN_UNROLL) == 0
    type_starts = _attn_block_types(n_rows)
    n_types = len(type_starts)
    qkv = qkv.reshape(N_ATTN_PARTS, ATTN_SLABS, batch, n_rows, HEAD_DIM)
    slopes = 2.0 ** (-8.0 * (jnp.arange(N_HEADS_ATTN, dtype=F32) + 1.0) / N_HEADS_ATTN)
    slopes = jnp.broadcast_to(slopes[:, None, None], (N_HEADS_ATTN, 1, HEAD_DIM))

    def cls(part, c):
        return pl.BlockSpec((None, None, None, n_rows, HEAD_DIM),
                            lambda h, b, r: (part, c * N_HEADS_ATTN + h, b, 0, 0))

    padded = n_rows + 2 * ATTN_PAD
    out = pl.pallas_call(
        functools.partial(_attn_kernel, n_rows=n_rows, type_starts=type_starts),
        grid=(N_HEADS_ATTN, batch, ATTN_CLASSES),
        in_specs=[pl.BlockSpec((None, 1, HEAD_DIM), lambda h, b, r: (h, 0, 0)),
                  pl.BlockSpec((None, None, None, n_rows, HEAD_DIM),
                               lambda h, b, r: (0, r * N_HEADS_ATTN + h, b, 0, 0))]
                 + [cls(1, c) for c in range(ATTN_CLASSES)] + [cls(2, c) for c in range(ATTN_CLASSES)],
        out_specs=pl.BlockSpec((None, None, n_rows, HEAD_DIM), lambda h, b, r: (r * N_HEADS_ATTN + h, b, 0, 0)),
        out_shape=jax.ShapeDtypeStruct((ATTN_SLABS, batch, n_rows, HEAD_DIM), BF16),
        scratch_shapes=[
            pltpu.VMEM((ATTN_CLASSES, padded, HEAD_DIM), BF16),
            pltpu.VMEM((ATTN_CLASSES, padded, HEAD_DIM), BF16),
            pltpu.VMEM((n_types, ATTN_BLOCK, ATTN_SAME), F32),
            pltpu.VMEM((ATTN_CLASSES, n_types, ATTN_CLASSES - 1, ATTN_BLOCK, ATTN_OTHER), F32),
        ],
        compiler_params=pltpu.CompilerParams(
            dimension_semantics=("arbitrary", "arbitrary", "arbitrary"), vmem_limit_bytes=VMEM_LIMIT_BYTES),
        name="attn",
    )(slopes, *([qkv] * (1 + 2 * ATTN_CLASSES)))
    return out.reshape(ATTN_SLABS, batch * n_rows, HEAD_DIM)


def _out_proj_kernel(hg_ref, at_ref, w_ref, x_ref, g_ref, b_ref, o_ref, y_ref):
    _store_lane_tiles(y_ref, ALPHA * x_ref[...]
                      + jnp.dot(hg_ref[...], w_ref[0:HGRN_WIDTH, :], preferred_element_type=F32))
    rows = at_ref.shape[1]
    n_tiles = y_ref.shape[0]
    for c in range(ATTN_CLASSES):
        at_c = jnp.concatenate([at_ref[c * N_HEADS_ATTN + h] for h in range(N_HEADS_ATTN)], axis=-1)
        mix = jnp.dot(at_c, w_ref[HGRN_WIDTH:HGRN_WIDTH + ATTN_WIDTH, :], preferred_element_type=F32)
        cls_rows = pl.ds(c, rows, stride=ATTN_CLASSES)
        y = jnp.concatenate([y_ref[j, cls_rows, :] for j in range(n_tiles)], axis=-1) + mix
        y = _layer_norm(y, g_ref[...], b_ref[...])
        for j in range(n_tiles):
            y_ref[j, cls_rows, :] = y[:, j * LANES:(j + 1) * LANES]
    o_ref[...] = jnp.concatenate([y_ref[j] for j in range(n_tiles)], axis=-1)


def _out_proj(hg, at4, w_out, x, g, b, *, tm=512):
    t, d = x.shape
    tok = lambda i: (i, 0)
    const = lambda i: (0, 0)
    return pl.pallas_call(
        _out_proj_kernel,
        grid=(t // tm,),
        in_specs=[
            pl.BlockSpec((tm, HGRN_WIDTH), tok),
            pl.BlockSpec((ATTN_SLABS, tm // ATTN_CLASSES, HEAD_DIM), lambda i: (0, i, 0)),
            pl.BlockSpec((HGRN_WIDTH + ATTN_WIDTH, d), const),
            pl.BlockSpec((tm, d), tok),
            pl.BlockSpec((1, d), const),
            pl.BlockSpec((1, d), const),
        ],
        out_specs=pl.BlockSpec((tm, d), tok),
        out_shape=jax.ShapeDtypeStruct((t, d), F32),
        scratch_shapes=[pltpu.VMEM((d // LANES, tm, LANES), F32)],
        compiler_params=pltpu.CompilerParams(
            dimension_semantics=("parallel",), vmem_limit_bytes=VMEM_LIMIT_BYTES),
        name="out_proj",
    )(hg, at4, w_out, x, g, b)


def _encoder_layer(x, w, layer):
    batch, seq, d = x.shape
    h = x.reshape(batch * seq, d)
    h, hb = _ffn(h, w["ffn1_w_gate"][layer], w["ffn1_w_up"][layer], w["ffn1_w_down"][layer],
                 w["ln1_g"][layer:layer + 1], w["ln1_b"][layer:layer + 1], emit_bf16=True)
    hg, qkv = _in_proj(hb, w["w_in"][layer])
    hgrn_out = _hgrn(hg, w["hgrn_lb_fwd"], w["hgrn_lb_bwd"], w["hgrn_norm_g"][layer], batch, seq)
    attn_out = _attn(qkv, batch, seq)
    h = _out_proj(hgrn_out.reshape(batch * seq, HGRN_WIDTH), attn_out, w["w_out"][layer], h,
                  w["ln2_g"][layer:layer + 1], w["ln2_b"][layer:layer + 1])
    h, = _ffn(h, w["ffn2_w_gate"][layer], w["ffn2_w_up"][layer], w["ffn2_w_down"][layer],
              w["ln3_g"][layer:layer + 1], w["ln3_b"][layer:layer + 1], emit_bf16=False)
    return h.reshape(batch, seq, d)


_MATMUL_WEIGHTS = ("ffn1_w_gate", "ffn1_w_up", "ffn1_w_down", "w_in", "w_out",
                   "ffn2_w_gate", "ffn2_w_up", "ffn2_w_down")


def kernel(x_prompt, x_sample, ln1_g, ln1_b, ffn1_w_gate, ffn1_w_up, ffn1_w_down, ln2_g, ln2_b, w_in,
           hgrn_lb_fwd, hgrn_lb_bwd, hgrn_norm_g, w_out, ln3_g, ln3_b, ffn2_w_gate, ffn2_w_up, ffn2_w_down):
    w = dict(ln1_g=ln1_g, ln1_b=ln1_b, ffn1_w_gate=ffn1_w_gate, ffn1_w_up=ffn1_w_up, ffn1_w_down=ffn1_w_down,
             ln2_g=ln2_g, ln2_b=ln2_b, w_in=w_in, hgrn_lb_fwd=hgrn_lb_fwd, hgrn_lb_bwd=hgrn_lb_bwd,
             hgrn_norm_g=hgrn_norm_g, w_out=w_out, ln3_g=ln3_g, ln3_b=ln3_b,
             ffn2_w_gate=ffn2_w_gate, ffn2_w_up=ffn2_w_up, ffn2_w_down=ffn2_w_down)
    for name in _MATMUL_WEIGHTS:
        w[name] = w[name].astype(BF16)
    outs = []
    for x in (x_prompt, x_sample):
        for layer in range(DEPTH):
            x = _encoder_layer(x, w, layer)
        outs.append(x)
    return tuple(outs)
```

```python
import functools

import numpy as np
import jax
import jax.numpy as jnp
from jax import lax
from jax.experimental import pallas as pl
from jax.experimental.pallas import tpu as pltpu

F32 = jnp.float32
BF16 = jnp.bfloat16

D_MODEL = 2048
D_FF = 5632
DEPTH = 1
HEAD_DIM = 128
N_HEADS_HGRN = 8
N_HEADS_ATTN = 8
HGRN_WIDTH = N_HEADS_HGRN * HEAD_DIM
ATTN_WIDTH = N_HEADS_ATTN * HEAD_DIM
IN_WIDTH = 5 * HGRN_WIDTH + 3 * ATTN_WIDTH
ALPHA = (2 * DEPTH) ** 0.25
LN_EPS = 1e-5
RMS_EPS = 1e-6
NEG_INF = -1e30

CHUNK = 64
HGRN_LEVELS = (1, 2, 4, 8, 16, 32)
N_LEVELS = len(HGRN_LEVELS)

ATTN_HALF = 64
ATTN_DILATIONS = (1, 4, 16)
ATTN_CLASSES = 4
ATTN_BLOCK = 128
ATTN_PAD = 256
ATTN_SAME = ATTN_BLOCK + 2 * ATTN_PAD
ATTN_OTHER_LEAD = 32
ATTN_OTHER = ATTN_BLOCK + 2 * ATTN_OTHER_LEAD
ATTN_UNROLL = 4
LANES = 128
SUBLANES = 8

VMEM_LIMIT_BYTES = 56 * 1024 * 1024
FFN_VMEM_LIMIT_BYTES = 60 * 1024 * 1024


def _layer_norm(y, g, b):
    mu = jnp.mean(y, axis=-1, keepdims=True)
    yc = y - mu
    var = jnp.mean(yc * yc, axis=-1, keepdims=True)
    return yc * lax.rsqrt(var + LN_EPS) * g + b


def _dot_nt(a, b):
    return lax.dot_general(a, b, (((1,), (1,)), ((), ())), preferred_element_type=F32)


def _dot_tn(a, b):
    return lax.dot_general(a, b, (((0,), (0,)), ((), ())), preferred_element_type=F32)


def _ffn_kernel(x_ref, wg_ref, wu_ref, wd_ref, g_ref, b_ref, *refs, n_ff_tiles, emit_bf16):
    if emit_bf16:
        o_ref, ob_ref, xb_ref = refs
    else:
        o_ref, xb_ref = refs
    j = pl.program_id(1)

    @pl.when(j == 0)
    def _():
        xb_ref[...] = x_ref[...].astype(BF16)
        o_ref[...] = jnp.zeros_like(o_ref)

    xb = xb_ref[...]
    gate = jnp.dot(xb, wg_ref[...], preferred_element_type=F32)
    up = jnp.dot(xb, wu_ref[...], preferred_element_type=F32)
    hidden = (gate * jax.nn.sigmoid(gate) * up).astype(BF16)
    o_ref[...] += jnp.dot(hidden, wd_ref[...], preferred_element_type=F32)

    @pl.when(j == n_ff_tiles - 1)
    def _():
        y = _layer_norm(ALPHA * x_ref[...] + 0.5 * o_ref[...], g_ref[...], b_ref[...])
        o_ref[...] = y
        if emit_bf16:
            ob_ref[...] = y.astype(BF16)


FFN_TILE = 256


def _column_tiles(w):
    *lead, d, d_ff = w.shape
    return jnp.swapaxes(w.reshape(*lead, d, d_ff // FFN_TILE, FFN_TILE), -3, -2)


def _ffn(x, wg, wu, wd, g, b, *, emit_bf16, tm=1024):
    t, d = x.shape
    n_ff_tiles, _, tf = wg.shape
    tok = lambda i, j: (i, 0)
    out_specs = [pl.BlockSpec((tm, d), tok)]
    out_shape = [jax.ShapeDtypeStruct((t, d), F32)]
    if emit_bf16:
        out_specs.append(pl.BlockSpec((tm, d), tok))
        out_shape.append(jax.ShapeDtypeStruct((t, d), BF16))
    return pl.pallas_call(
        functools.partial(_ffn_kernel, n_ff_tiles=n_ff_tiles, emit_bf16=emit_bf16),
        grid=(t // tm, n_ff_tiles),
        in_specs=[
            pl.BlockSpec((tm, d), tok),
            pl.BlockSpec((None, d, tf), lambda i, j: (j, 0, 0)),
            pl.BlockSpec((None, d, tf), lambda i, j: (j, 0, 0)),
            pl.BlockSpec((tf, d), lambda i, j: (j, 0)),
            pl.BlockSpec((1, d), lambda i, j: (0, 0)),
            pl.BlockSpec((1, d), lambda i, j: (0, 0)),
        ],
        out_specs=out_specs,
        out_shape=out_shape,
        scratch_shapes=[pltpu.VMEM((tm, d), BF16)],
        compiler_params=pltpu.CompilerParams(
            dimension_semantics=("parallel", "arbitrary"), vmem_limit_bytes=FFN_VMEM_LIMIT_BYTES),
        name="ffn",
    )(x, wg, wu, wd, g, b)


N_HGRN_PARTS = 5
N_ATTN_PARTS = 3
ATTN_SLABS = ATTN_CLASSES * N_HEADS_ATTN
IN_TILE = 1024


def _proj_heads_kernel(x_ref, w_ref, o_ref):
    r = jnp.dot(x_ref[...], w_ref[...], preferred_element_type=F32)
    for c in range(N_HEADS_HGRN):
        o_ref[c] = r[:, c * HEAD_DIM:(c + 1) * HEAD_DIM]


def _store_lane_tiles(ref, x):
    for j in range(ref.shape[0]):
        ref[j] = x[:, j * LANES:(j + 1) * LANES]


def _proj_classes_kernel(x_ref, w_ref, o_ref, r_ref):
    _store_lane_tiles(r_ref, jnp.dot(x_ref[...], w_ref[...], preferred_element_type=F32))
    rows = o_ref.shape[1]
    for c in range(ATTN_CLASSES):
        cls_rows = pl.ds(c, rows, stride=ATTN_CLASSES)
        for h in range(N_HEADS_ATTN):
            o_ref[c * N_HEADS_ATTN + h] = r_ref[h, cls_rows, :].astype(BF16)


def _in_proj(xb, w_in, *, tm=1024):
    t, d = xb.shape
    params = pltpu.CompilerParams(dimension_semantics=("parallel", "arbitrary"), vmem_limit_bytes=VMEM_LIMIT_BYTES)
    hg = pl.pallas_call(
        _proj_heads_kernel,
        grid=(t // tm, N_HGRN_PARTS),
        in_specs=[pl.BlockSpec((tm, d), lambda i, j: (i, 0)), pl.BlockSpec((d, IN_TILE), lambda i, j: (0, j))],
        out_specs=pl.BlockSpec((N_HEADS_HGRN, tm, HEAD_DIM), lambda i, j: (j, i, 0)),
        out_shape=jax.ShapeDtypeStruct((N_HGRN_PARTS * N_HEADS_HGRN, t, HEAD_DIM), F32),
        compiler_params=params,
        name="in_proj_hgrn",
    )(xb, w_in)
    qkv = pl.pallas_call(
        _proj_classes_kernel,
        grid=(t // tm, N_ATTN_PARTS),
        in_specs=[pl.BlockSpec((tm, d), lambda i, j: (i, 0)),
                  pl.BlockSpec((d, IN_TILE), lambda i, j: (0, N_HGRN_PARTS + j))],
        out_specs=pl.BlockSpec((None, ATTN_SLABS, tm // ATTN_CLASSES, HEAD_DIM), lambda i, j: (j, 0, i, 0)),
        out_shape=jax.ShapeDtypeStruct((N_ATTN_PARTS, ATTN_SLABS, t // ATTN_CLASSES, HEAD_DIM), BF16),
        scratch_shapes=[pltpu.VMEM((IN_TILE // LANES, tm, LANES), F32)],
        compiler_params=params,
        name="in_proj_attn",
    )(xb, w_in)
    return hg, qkv


def _hgrn_masks():
    t = np.arange(CHUNK)[:, None]
    u = np.arange(CHUNK)[None, :]
    masks = [(t == u)]
    for c in HGRN_LEVELS:
        same = (t // (2 * c)) == (u // (2 * c))
        masks.append(same & ((t % (2 * c)) >= c) & ((u % (2 * c)) < c))
    return np.stack(masks, axis=0).astype(np.float32)


def _rows(x, r, n):
    return jnp.broadcast_to(x[r:r + 1], (n, x.shape[-1]))


def _chunk_cumsum(x, reverse):
    sub = lax.broadcasted_iota(jnp.int32, x.shape, 0) % SUBLANES
    step = 1
    while step < SUBLANES:
        if reverse:
            x = x + jnp.where(sub < SUBLANES - step, pltpu.roll(x, CHUNK - step, axis=0), 0.0)
        else:
            x = x + jnp.where(sub >= step, pltpu.roll(x, step, axis=0), 0.0)
        step *= 2
    tiles = [x[j:j + SUBLANES] for j in range(0, CHUNK, SUBLANES)]
    if reverse:
        tiles = tiles[::-1]
    edge = 0 if reverse else SUBLANES - 1
    for j in range(1, len(tiles)):
        tiles[j] = tiles[j] + _rows(tiles[j - 1], edge, SUBLANES)
    if reverse:
        tiles = tiles[::-1]
    return jnp.concatenate(tiles, axis=0)


def _query_side(c, reverse):
    t = np.arange(CHUNK)
    return ((t % (2 * c)) >= c) != reverse


def _hgrn_signs(reverse):
    log2e = np.float32(np.log2(np.e))
    sign = np.stack([np.where(_query_side(c, reverse), log2e, -log2e) for c in HGRN_LEVELS])
    return np.broadcast_to(sign[:, :, None], (N_LEVELS, CHUNK, HEAD_DIM)).astype(np.float32)


def _level_boundary(g, c, reverse):
    first = c if reverse else c - 1
    if 2 * c >= SUBLANES:
        return jnp.concatenate([_rows(g, b, 2 * c) for b in range(first, CHUNK, 2 * c)], axis=0)
    sub = lax.broadcasted_iota(jnp.int32, (SUBLANES, g.shape[-1]), 0)
    return jnp.concatenate(
        [jnp.where(sub < 2 * c, _rows(g, j + first, SUBLANES), _rows(g, j + 2 * c + first, SUBLANES))
         for j in range(0, CHUNK, SUBLANES)], axis=0)


def _hgrn_chunk(q, z, v, lb, masks, signs, state_ref, reverse):
    f = lb + (1.0 - lb) * jax.nn.sigmoid(z)
    k = 1.0 - f
    g_incl = _chunk_cumsum(jnp.log(f), reverse)
    last_row = 0 if reverse else CHUNK - 1
    decay_in = jnp.exp(g_incl)
    decay_out = jnp.exp(_rows(g_incl, last_row, CHUNK) - g_incl)
    decay_all = jnp.exp(g_incl[last_row:last_row + 1])

    qb, kb, vb = q.astype(BF16), k.astype(BF16), v.astype(BF16)
    diag = _dot_nt(qb, kb) * masks[0]
    a = [diag[j:j + SUBLANES] for j in range(0, CHUNK, SUBLANES)]
    sub = lax.broadcasted_iota(jnp.int32, q.shape, 0)
    for lv, c in enumerate(HGRN_LEVELS):
        mask = masks[lv + 1]
        query_side = _query_side(c, reverse)
        if c < SUBLANES:
            rows_q = ((sub % (2 * c)) >= c) != reverse
            if c == 1:
                mixed = jnp.where(rows_q, q * f, k)
            else:
                decay = jnp.exp2((g_incl - _level_boundary(g_incl, c, reverse)) * signs[lv])
                mixed = jnp.where(rows_q, q, k) * decay
            mixed = mixed.astype(BF16)
            prod = _dot_nt(mixed, mixed) * mask
            a = [a[j] + prod[j * SUBLANES:(j + 1) * SUBLANES] for j in range(len(a))]
            continue
        decay = jnp.exp2((g_incl - _level_boundary(g_incl, c, reverse)) * signs[lv])
        mixed = jnp.concatenate([(q if query_side[r] else k)[r:r + c] for r in range(0, CHUNK, c)], axis=0) * decay
        q_rows = [r for r in range(0, CHUNK, SUBLANES) if query_side[r]]
        lhs = jnp.concatenate([mixed[r:r + SUBLANES] for r in q_rows], axis=0).astype(BF16)
        prod = _dot_nt(lhs, mixed.astype(BF16))
        for i, r in enumerate(q_rows):
            j = r // SUBLANES
            a[j] = a[j] + prod[i * SUBLANES:(i + 1) * SUBLANES] * mask[r:r + SUBLANES]
    a = jnp.concatenate(a, axis=0)
    state = state_ref[...]
    out = jnp.dot(a.astype(BF16), vb, preferred_element_type=F32)
    out = out + _dot_nt((q * decay_in).astype(BF16), state.astype(BF16))
    state_ref[...] = state * decay_all + _dot_tn(vb, (k * decay_out).astype(BF16))
    return out


def _hgrn_kernel(qf_ref, zf_ref, vf_ref, gf_ref, qb_ref, zb_ref, vb_ref, gb_ref, lbf_ref, lbb_ref, ng_ref,
                 mf_ref, mb_ref, sgf_ref, sgb_ref, o_ref, part_ref, stf_ref, stb_ref, *, n_tiles, tile):
    s = pl.program_id(2)
    n_heads = qf_ref.shape[0]
    chunks = tile // CHUNK

    @pl.when(s == 0)
    def _():
        stf_ref[...] = jnp.zeros_like(stf_ref)
        stb_ref[...] = jnp.zeros_like(stb_ref)

    def lower_bound(ref, h):
        p = ref[h]
        p = jnp.exp(p - jnp.max(p, axis=0, keepdims=True))
        return p[0:1] / jnp.sum(p, axis=0, keepdims=True)

    def scan(finish):
        def scan_step(c, carry):
            lf = pl.multiple_of(c * CHUNK, CHUNK)
            lb = pl.multiple_of((chunks - 1 - c) * CHUNK, CHUNK)
            rows_f = pl.ds(lf, CHUNK)
            rows_b = pl.ds(lb, CHUNK)
            out_f = pl.ds(pl.multiple_of(s * tile + lf, CHUNK), CHUNK)
            out_b = pl.ds(pl.multiple_of((n_tiles - 1 - s) * tile + lb, CHUNK), CHUNK)
            res = []
            for h in range(n_heads):
                res.append(_hgrn_chunk(qf_ref[h, rows_f, :], zf_ref[h, rows_f, :], vf_ref[h, rows_f, :],
                                       lower_bound(lbf_ref, h), mf_ref[...], sgf_ref[...], stf_ref.at[h], False))
                res.append(_hgrn_chunk(qb_ref[h, rows_b, :], zb_ref[h, rows_b, :], vb_ref[h, rows_b, :],
                                       lower_bound(lbb_ref, h), mb_ref[...], sgb_ref[...], stb_ref.at[h], True))
            for h in range(n_heads):
                for o, rows, out_rows, g_ref in ((res[2 * h], rows_f, out_f, gf_ref),
                                                 (res[2 * h + 1], rows_b, out_b, gb_ref)):
                    if not finish:
                        part_ref[h, out_rows, :] = o
                        continue
                    o = o + part_ref[h, out_rows, :]
                    o = o * lax.rsqrt(jnp.mean(o * o, axis=-1, keepdims=True) + RMS_EPS)
                    gate = g_ref[h, rows, :]
                    o = o * ng_ref[h] * (gate * jax.nn.sigmoid(gate))
                    o_ref[out_rows, h * HEAD_DIM:(h + 1) * HEAD_DIM] = o.astype(BF16)
            return carry

        lax.fori_loop(0, chunks, scan_step, 0, unroll=HGRN_UNROLL)

    @pl.when(s < n_tiles // 2)
    def _():
        scan(finish=False)

    @pl.when(s >= n_tiles // 2)
    def _():
        scan(finish=True)


HGRN_HEADS_PER_STEP = 4
HGRN_TILE = 512
HGRN_UNROLL = 4


def _hgrn(hg, lb_fwd, lb_bwd, norm_g, batch, seq):
    hg5 = hg.reshape(N_HGRN_PARTS, N_HEADS_HGRN, batch, seq, HEAD_DIM)
    masks = _hgrn_masks()
    masks_f = jnp.asarray(masks, F32)
    masks_b = jnp.asarray(masks[:, ::-1, ::-1], F32)
    hps, tile = HGRN_HEADS_PER_STEP, HGRN_TILE
    n_tiles = seq // tile
    assert seq % tile == 0 and n_tiles % 2 == 0 and N_HEADS_HGRN % hps == 0

    def part(p, backward):
        def index(b, hg_, s):
            return (p, hg_, b, (n_tiles - 1 - s) if backward else s, 0)
        return pl.BlockSpec((None, hps, None, tile, HEAD_DIM), index)

    def per_head(rows):
        return pl.BlockSpec((hps, rows, HEAD_DIM), lambda b, hg_, s: (hg_, 0, 0))

    def lb_per_head(lb):
        return lb.reshape(DEPTH + 1, N_HEADS_HGRN, HEAD_DIM).transpose(1, 0, 2)

    const3 = lambda b, hg_, s: (0, 0, 0)
    q, zf, zb, v, g = range(N_HGRN_PARTS)
    return pl.pallas_call(
        functools.partial(_hgrn_kernel, n_tiles=n_tiles, tile=tile),
        grid=(batch, N_HEADS_HGRN // hps, n_tiles),
        in_specs=[
            part(q, False), part(zf, False), part(v, False), part(g, False),
            part(q, True), part(zb, True), part(v, True), part(g, True),
            per_head(DEPTH + 1), per_head(DEPTH + 1), per_head(1),
            pl.BlockSpec((N_LEVELS + 1, CHUNK, CHUNK), const3),
            pl.BlockSpec((N_LEVELS + 1, CHUNK, CHUNK), const3),
            pl.BlockSpec((N_LEVELS, CHUNK, HEAD_DIM), const3),
            pl.BlockSpec((N_LEVELS, CHUNK, HEAD_DIM), const3),
        ],
        out_specs=pl.BlockSpec((None, seq, hps * HEAD_DIM), lambda b, hg_, s: (b, 0, hg_)),
        out_shape=jax.ShapeDtypeStruct((batch, seq, HGRN_WIDTH), BF16),
        scratch_shapes=[
            pltpu.VMEM((hps, seq, HEAD_DIM), F32),
            pltpu.VMEM((hps, HEAD_DIM, HEAD_DIM), F32), pltpu.VMEM((hps, HEAD_DIM, HEAD_DIM), F32),
        ],
        compiler_params=pltpu.CompilerParams(
            dimension_semantics=("parallel", "parallel", "arbitrary"), vmem_limit_bytes=VMEM_LIMIT_BYTES),
        name="hgrn",
    )(hg5, hg5, hg5, hg5, hg5, hg5, hg5, hg5,
      lb_per_head(lb_fwd), lb_per_head(lb_bwd), norm_g.reshape(N_HEADS_HGRN, 1, HEAD_DIM),
      masks_f, masks_b, jnp.asarray(_hgrn_signs(False)), jnp.asarray(_hgrn_signs(True)))


def _attn_block_types(n_rows):
    row = np.arange(ATTN_BLOCK)

    def flags(n):
        i = n * ATTN_BLOCK + row
        reach = [ATTN_HALF * d // ATTN_CLASSES for d in ATTN_DILATIONS]
        return np.stack([i < r for r in reach] + [i >= n_rows - r for r in reach]).tobytes()

    starts = [0]
    for n in range(1, n_rows // ATTN_BLOCK):
        if flags(n) != flags(n - 1):
            starts.append(n)
    return tuple(starts)


def _attn_bias(delta, i_query, slope, n_rows):
    mult = jnp.zeros(delta.shape, F32)
    for d in ATTN_DILATIONS:
        reach = ATTN_HALF * d // ATTN_CLASSES
        lo = jnp.where(i_query < reach, 0, -ATTN_HALF * d)
        hi = jnp.where(i_query >= n_rows - reach, (ATTN_HALF - 1) * d, ATTN_HALF * d)
        hit = (delta >= lo) & (delta <= hi)
        if d > 1:
            hit = hit & (lax.rem(delta, d) == 0)
        mult = mult + hit.astype(F32)
    dist = jnp.abs(delta).astype(F32)
    return jnp.where(mult > 0.0, jnp.log(jnp.maximum(mult, 1.0)) - slope * dist, NEG_INF)


def _lane_reduce(parts, combine, reduce):
    folded = {}
    for part in parts:
        for j in range(0, part.shape[-1], LANES):
            tile = part[:, j:j + LANES]
            w = tile.shape[-1]
            folded[w] = tile if w not in folded else combine(folded[w], tile)
    out = None
    for tile in folded.values():
        r = reduce(tile, axis=-1, keepdims=True)
        out = r if out is None else combine(out, r)
    return out


def _attn_kernel(slope_ref, q_ref, k0_ref, k1_ref, k2_ref, k3_ref, v0_ref, v1_ref, v2_ref, v3_ref,
                 o_ref, kpad_ref, vpad_ref, bias_same_ref, bias_other_ref, *, n_rows, type_starts):
    b, rq = pl.program_id(1), pl.program_id(2)

    @pl.when((b == 0) & (rq == 0))
    def _():
        slope = slope_ref[...][:, 0:1]
        row = lax.broadcasted_iota(jnp.int32, (ATTN_BLOCK, ATTN_SAME), 0)
        col = lax.broadcasted_iota(jnp.int32, (ATTN_BLOCK, ATTN_SAME), 1)
        for ty, first_block in enumerate(type_starts):
            bias_same_ref[ty] = _attn_bias((col - ATTN_PAD - row) * ATTN_CLASSES,
                                           first_block * ATTN_BLOCK + row, slope, n_rows)
        row = lax.broadcasted_iota(jnp.int32, (ATTN_BLOCK, ATTN_OTHER), 0)
        col = lax.broadcasted_iota(jnp.int32, (ATTN_BLOCK, ATTN_OTHER), 1)
        for query_class in range(ATTN_CLASSES):
            for m in range(1, ATTN_CLASSES):
                delta = (col - ATTN_OTHER_LEAD - row) * ATTN_CLASSES + ((query_class + m) % ATTN_CLASSES - query_class)
                for ty, first_block in enumerate(type_starts):
                    bias_other_ref[query_class, ty, m - 1] = _attn_bias(
                        delta, first_block * ATTN_BLOCK + row, slope, n_rows)

    @pl.when(rq == 0)
    def _():
        zeros = jnp.zeros((ATTN_PAD, HEAD_DIM), BF16)
        for c, (k_ref, v_ref) in enumerate(((k0_ref, v0_ref), (k1_ref, v1_ref), (k2_ref, v2_ref), (k3_ref, v3_ref))):
            for pad_ref, src_ref in ((kpad_ref, k_ref), (vpad_ref, v_ref)):
                pad_ref[c, 0:ATTN_PAD, :] = zeros
                pad_ref[c, ATTN_PAD:ATTN_PAD + n_rows, :] = src_ref[...]
                pad_ref[c, ATTN_PAD + n_rows:ATTN_PAD + n_rows + ATTN_PAD, :] = zeros

    scale = HEAD_DIM ** -0.5

    def block(n, carry):
        i0 = pl.multiple_of(n * ATTN_BLOCK, ATTN_BLOCK)
        ty = jnp.int32(0)
        for first_block in type_starts[1:]:
            ty = ty + jnp.asarray(n >= first_block, jnp.int32)
        qb = q_ref[pl.ds(i0, ATTN_BLOCK), :]
        scores, values = [], []
        s = _dot_nt(qb, kpad_ref[rq, pl.ds(i0, ATTN_SAME), :]) * scale
        scores.append(s + bias_same_ref[ty])
        values.append(vpad_ref[rq, pl.ds(i0, ATTN_SAME), :])
        start = pl.multiple_of(i0 + (ATTN_PAD - ATTN_OTHER_LEAD), ATTN_OTHER_LEAD)
        for m in range(1, ATTN_CLASSES):
            rk = lax.rem(rq + m, ATTN_CLASSES)
            s = _dot_nt(qb, kpad_ref[rk, pl.ds(start, ATTN_OTHER), :]) * scale
            scores.append(s + bias_other_ref[rq, ty, m - 1])
            values.append(vpad_ref[rk, pl.ds(start, ATTN_OTHER), :])
        mx = _lane_reduce(scores, jnp.maximum, jnp.max)
        probs = [jnp.exp(s - mx) for s in scores]
        den = _lane_reduce(probs, jnp.add, jnp.sum)
        acc = jnp.zeros((ATTN_BLOCK, HEAD_DIM), F32)
        for p, v in zip(probs, values):
            acc = acc + jnp.dot(p.astype(BF16), v, preferred_element_type=F32)
        o_ref[pl.ds(i0, ATTN_BLOCK), :] = (acc / den).astype(BF16)
        return carry

    lax.fori_loop(0, n_rows // ATTN_BLOCK, block, 0, unroll=ATTN_UNROLL)


def _attn(qkv, batch, seq):
    n_rows = seq // ATTN_CLASSES
    assert seq % (ATTN_HALF * max(ATTN_DILATIONS)) == 0 and seq >= 2 * ATTN_HALF * max(ATTN_DILATIONS)
    assert n_rows % (ATTN_BLOCK * ATTN_UNROLL) == 0
    type_starts = _attn_block_types(n_rows)
    n_types = len(type_starts)
    qkv = qkv.reshape(N_ATTN_PARTS, ATTN_SLABS, batch, n_rows, HEAD_DIM)
    slopes = 2.0 ** (-8.0 * (jnp.arange(N_HEADS_ATTN, dtype=F32) + 1.0) / N_HEADS_ATTN)
    slopes = jnp.broadcast_to(slopes[:, None, None], (N_HEADS_ATTN, 1, HEAD_DIM))

    def cls(part, c):
        return pl.BlockSpec((None, None, None, n_rows, HEAD_DIM),
                            lambda h, b, r: (part, c * N_HEADS_ATTN + h, b, 0, 0))

    padded = n_rows + 2 * ATTN_PAD
    out = pl.pallas_call(
        functools.partial(_attn_kernel, n_rows=n_rows, type_starts=type_starts),
        grid=(N_HEADS_ATTN, batch, ATTN_CLASSES),
        in_specs=[pl.BlockSpec((None, 1, HEAD_DIM), lambda h, b, r: (h, 0, 0)),
                  pl.BlockSpec((None, None, None, n_rows, HEAD_DIM),
                               lambda h, b, r: (0, r * N_HEADS_ATTN + h, b, 0, 0))]
                 + [cls(1, c) for c in range(ATTN_CLASSES)] + [cls(2, c) for c in range(ATTN_CLASSES)],
        out_specs=pl.BlockSpec((None, None, n_rows, HEAD_DIM), lambda h, b, r: (r * N_HEADS_ATTN + h, b, 0, 0)),
        out_shape=jax.ShapeDtypeStruct((ATTN_SLABS, batch, n_rows, HEAD_DIM), BF16),
        scratch_shapes=[
            pltpu.VMEM((ATTN_CLASSES, padded, HEAD_DIM), BF16),
            pltpu.VMEM((ATTN_CLASSES, padded, HEAD_DIM), BF16),
            pltpu.VMEM((n_types, ATTN_BLOCK, ATTN_SAME), F32),
            pltpu.VMEM((ATTN_CLASSES, n_types, ATTN_CLASSES - 1, ATTN_BLOCK, ATTN_OTHER), F32),
        ],
        compiler_params=pltpu.CompilerParams(
            dimension_semantics=("arbitrary", "arbitrary", "arbitrary"), vmem_limit_bytes=VMEM_LIMIT_BYTES),
        name="attn",
    )(slopes, *([qkv] * (1 + 2 * ATTN_CLASSES)))
    return out.reshape(ATTN_SLABS, batch * n_rows, HEAD_DIM)


def _out_proj_kernel(hg_ref, at_ref, w_ref, x_ref, g_ref, b_ref, o_ref, y_ref):
    _store_lane_tiles(y_ref, ALPHA * x_ref[...]
                      + jnp.dot(hg_ref[...], w_ref[0:HGRN_WIDTH, :], preferred_element_type=F32))
    rows = at_ref.shape[1]
    n_tiles = y_ref.shape[0]
    for c in range(ATTN_CLASSES):
        at_c = jnp.concatenate([at_ref[c * N_HEADS_ATTN + h] for h in range(N_HEADS_ATTN)], axis=-1)
        mix = jnp.dot(at_c, w_ref[HGRN_WIDTH:HGRN_WIDTH + ATTN_WIDTH, :], preferred_element_type=F32)
        cls_rows = pl.ds(c, rows, stride=ATTN_CLASSES)
        y = jnp.concatenate([y_ref[j, cls_rows, :] for j in range(n_tiles)], axis=-1) + mix
        y = _layer_norm(y, g_ref[...], b_ref[...])
        for j in range(n_tiles):
            y_ref[j, cls_rows, :] = y[:, j * LANES:(j + 1) * LANES]
    o_ref[...] = jnp.concatenate([y_ref[j] for j in range(n_tiles)], axis=-1)


def _out_proj(hg, at4, w_out, x, g, b, *, tm=512):
    t, d = x.shape
    tok = lambda i: (i, 0)
    const = lambda i: (0, 0)
    return pl.pallas_call(
        _out_proj_kernel,
        grid=(t // tm,),
        in_specs=[
            pl.BlockSpec((tm, HGRN_WIDTH), tok),
            pl.BlockSpec((ATTN_SLABS, tm // ATTN_CLASSES, HEAD_DIM), lambda i: (0, i, 0)),
            pl.BlockSpec((HGRN_WIDTH + ATTN_WIDTH, d), const),
            pl.BlockSpec((tm, d), tok),
            pl.BlockSpec((1, d), const),
            pl.BlockSpec((1, d), const),
        ],
        out_specs=pl.BlockSpec((tm, d), tok),
        out_shape=jax.ShapeDtypeStruct((t, d), F32),
        scratch_shapes=[pltpu.VMEM((d // LANES, tm, LANES), F32)],
        compiler_params=pltpu.CompilerParams(
            dimension_semantics=("parallel",), vmem_limit_bytes=VMEM_LIMIT_BYTES),
        name="out_proj",
    )(hg, at4, w_out, x, g, b)


def _encoder_layer(x, w, layer):
    batch, seq, d = x.shape
    h = x.reshape(batch * seq, d)
    h, hb = _ffn(h, w["ffn1_w_gate"][layer], w["ffn1_w_up"][layer], w["ffn1_w_down"][layer],
                 w["ln1_g"][layer:layer + 1], w["ln1_b"][layer:layer + 1], emit_bf16=True)
    hg, qkv = _in_proj(hb, w["w_in"][layer])
    hgrn_out = _hgrn(hg, w["hgrn_lb_fwd"], w["hgrn_lb_bwd"], w["hgrn_norm_g"][layer], batch, seq)
    attn_out = _attn(qkv, batch, seq)
    h = _out_proj(hgrn_out.reshape(batch * seq, HGRN_WIDTH), attn_out, w["w_out"][layer], h,
                  w["ln2_g"][layer:layer + 1], w["ln2_b"][layer:layer + 1])
    h, = _ffn(h, w["ffn2_w_gate"][layer], w["ffn2_w_up"][layer], w["ffn2_w_down"][layer],
              w["ln3_g"][layer:layer + 1], w["ln3_b"][layer:layer + 1], emit_bf16=False)
    return h.reshape(batch, seq, d)


_MATMUL_WEIGHTS = ("ffn1_w_gate", "ffn1_w_up", "ffn1_w_down", "w_in", "w_out",
                   "ffn2_w_gate", "ffn2_w_up", "ffn2_w_down")


def kernel(x_prompt, x_sample, ln1_g, ln1_b, ffn1_w_gate, ffn1_w_up, ffn1_w_down, ln2_g, ln2_b, w_in,
           hgrn_lb_fwd, hgrn_lb_bwd, hgrn_norm_g, w_out, ln3_g, ln3_b, ffn2_w_gate, ffn2_w_up, ffn2_w_down):
    w = dict(ln1_g=ln1_g, ln1_b=ln1_b, ffn1_w_gate=ffn1_w_gate, ffn1_w_up=ffn1_w_up, ffn1_w_down=ffn1_w_down,
             ln2_g=ln2_g, ln2_b=ln2_b, w_in=w_in, hgrn_lb_fwd=hgrn_lb_fwd, hgrn_lb_bwd=hgrn_lb_bwd,
             hgrn_norm_g=hgrn_norm_g, w_out=w_out, ln3_g=ln3_g, ln3_b=ln3_b,
             ffn2_w_gate=ffn2_w_gate, ffn2_w_up=ffn2_w_up, ffn2_w_down=ffn2_w_down)
    for name in _MATMUL_WEIGHTS:
        w[name] = w[name].astype(BF16)
    for name in ("ffn1_w_gate", "ffn1_w_up", "ffn2_w_gate", "ffn2_w_up"):
        w[name] = _column_tiles(w[name])
    outs = []
    for x in (x_prompt, x_sample):
        for layer in range(DEPTH):
            x = _encoder_layer(x, w, layer)
        outs.append(x)
    return tuple(outs)
```

```python
import functools

import numpy as np
import jax
import jax.numpy as jnp
from jax import lax
from jax.experimental import pallas as pl
from jax.experimental.pallas import tpu as pltpu

F32 = jnp.float32
BF16 = jnp.bfloat16

D_MODEL = 2048
D_FF = 5632
DEPTH = 1
HEAD_DIM = 128
N_HEADS_HGRN = 8
N_HEADS_ATTN = 8
HGRN_WIDTH = N_HEADS_HGRN * HEAD_DIM
ATTN_WIDTH = N_HEADS_ATTN * HEAD_DIM
IN_WIDTH = 5 * HGRN_WIDTH + 3 * ATTN_WIDTH
ALPHA = (2 * DEPTH) ** 0.25
LN_EPS = 1e-5
RMS_EPS = 1e-6
NEG_INF = -1e30

CHUNK = 64
HGRN_LEVELS = (1, 2, 4, 8, 16, 32)
N_LEVELS = len(HGRN_LEVELS)

ATTN_HALF = 64
ATTN_DILATIONS = (1, 4, 16)
ATTN_CLASSES = 4
ATTN_BLOCK = 128
ATTN_PAD = 256
ATTN_SAME = ATTN_BLOCK + 2 * ATTN_PAD
ATTN_OTHER_LEAD = 32
ATTN_OTHER = ATTN_BLOCK + 2 * ATTN_OTHER_LEAD
ATTN_UNROLL = 4
LANES = 128
SUBLANES = 8

VMEM_LIMIT_BYTES = 56 * 1024 * 1024
FFN_VMEM_LIMIT_BYTES = 60 * 1024 * 1024


def _layer_norm(y, g, b):
    mu = jnp.mean(y, axis=-1, keepdims=True)
    yc = y - mu
    var = jnp.mean(yc * yc, axis=-1, keepdims=True)
    return yc * lax.rsqrt(var + LN_EPS) * g + b


def _dot_nt(a, b):
    return lax.dot_general(a, b, (((1,), (1,)), ((), ())), preferred_element_type=F32)


def _dot_tn(a, b):
    return lax.dot_general(a, b, (((0,), (0,)), ((), ())), preferred_element_type=F32)


def _ffn_kernel(x_ref, wg_ref, wu_ref, wd_ref, g_ref, b_ref, *refs, n_ff_tiles, emit_bf16):
    if emit_bf16:
        o_ref, ob_ref, xb_ref = refs
    else:
        o_ref, xb_ref = refs
    j = pl.program_id(1)

    @pl.when(j == 0)
    def _():
        xb_ref[...] = x_ref[...].astype(BF16)
        o_ref[...] = jnp.zeros_like(o_ref)

    xb = xb_ref[...]
    gate = jnp.dot(xb, wg_ref[...], preferred_element_type=F32)
    up = jnp.dot(xb, wu_ref[...], preferred_element_type=F32)
    hidden = (gate * jax.nn.sigmoid(gate) * up).astype(BF16)
    o_ref[...] += jnp.dot(hidden, wd_ref[...], preferred_element_type=F32)

    @pl.when(j == n_ff_tiles - 1)
    def _():
        y = _layer_norm(ALPHA * x_ref[...] + 0.5 * o_ref[...], g_ref[...], b_ref[...])
        o_ref[...] = y
        if emit_bf16:
            ob_ref[...] = y.astype(BF16)


def _ffn(x, wg, wu, wd, g, b, *, emit_bf16, tm=1024, tf=256):
    t, d = x.shape
    n_ff_tiles = D_FF // tf
    tok = lambda i, j: (i, 0)
    out_specs = [pl.BlockSpec((tm, d), tok)]
    out_shape = [jax.ShapeDtypeStruct((t, d), F32)]
    if emit_bf16:
        out_specs.append(pl.BlockSpec((tm, d), tok))
        out_shape.append(jax.ShapeDtypeStruct((t, d), BF16))
    return pl.pallas_call(
        functools.partial(_ffn_kernel, n_ff_tiles=n_ff_tiles, emit_bf16=emit_bf16),
        grid=(t // tm, n_ff_tiles),
        in_specs=[
            pl.BlockSpec((tm, d), tok),
            pl.BlockSpec((d, tf), lambda i, j: (0, j)),
            pl.BlockSpec((d, tf), lambda i, j: (0, j)),
            pl.BlockSpec((tf, d), lambda i, j: (j, 0)),
            pl.BlockSpec((1, d), lambda i, j: (0, 0)),
            pl.BlockSpec((1, d), lambda i, j: (0, 0)),
        ],
        out_specs=out_specs,
        out_shape=out_shape,
        scratch_shapes=[pltpu.VMEM((tm, d), BF16)],
        compiler_params=pltpu.CompilerParams(
            dimension_semantics=("parallel", "arbitrary"), vmem_limit_bytes=FFN_VMEM_LIMIT_BYTES),
        name="ffn",
    )(x, wg, wu, wd, g, b)


N_HGRN_PARTS = 5
N_ATTN_PARTS = 3
ATTN_SLABS = N_HEADS_ATTN * ATTN_CLASSES
IN_TILE = 1024


def _proj_heads_kernel(x_ref, w_ref, o_ref):
    r = jnp.dot(x_ref[...], w_ref[...], preferred_element_type=F32)
    for c in range(N_HEADS_HGRN):
        o_ref[c] = r[:, c * HEAD_DIM:(c + 1) * HEAD_DIM]


def _store_lane_tiles(ref, x):
    for j in range(ref.shape[0]):
        ref[j] = x[:, j * LANES:(j + 1) * LANES]


def _proj_classes_kernel(x_ref, w_ref, o_ref, r_ref):
    _store_lane_tiles(r_ref, jnp.dot(x_ref[...], w_ref[...], preferred_element_type=F32))
    rows = o_ref.shape[1]
    for c in range(ATTN_CLASSES):
        cls_rows = pl.ds(c, rows, stride=ATTN_CLASSES)
        for h in range(N_HEADS_ATTN):
            o_ref[h * ATTN_CLASSES + c] = r_ref[h, cls_rows, :].astype(BF16)


def _in_proj(xb, w_in, *, tm=1024):
    t, d = xb.shape
    params = pltpu.CompilerParams(dimension_semantics=("parallel", "arbitrary"), vmem_limit_bytes=VMEM_LIMIT_BYTES)
    hg = pl.pallas_call(
        _proj_heads_kernel,
        grid=(t // tm, N_HGRN_PARTS),
        in_specs=[pl.BlockSpec((tm, d), lambda i, j: (i, 0)), pl.BlockSpec((d, IN_TILE), lambda i, j: (0, j))],
        out_specs=pl.BlockSpec((N_HEADS_HGRN, tm, HEAD_DIM), lambda i, j: (j, i, 0)),
        out_shape=jax.ShapeDtypeStruct((N_HGRN_PARTS * N_HEADS_HGRN, t, HEAD_DIM), F32),
        compiler_params=params,
        name="in_proj_hgrn",
    )(xb, w_in)
    qkv = pl.pallas_call(
        _proj_classes_kernel,
        grid=(t // tm, N_ATTN_PARTS),
        in_specs=[pl.BlockSpec((tm, d), lambda i, j: (i, 0)),
                  pl.BlockSpec((d, IN_TILE), lambda i, j: (0, N_HGRN_PARTS + j))],
        out_specs=pl.BlockSpec((None, ATTN_SLABS, tm // ATTN_CLASSES, HEAD_DIM), lambda i, j: (j, 0, i, 0)),
        out_shape=jax.ShapeDtypeStruct((N_ATTN_PARTS, ATTN_SLABS, t // ATTN_CLASSES, HEAD_DIM), BF16),
        scratch_shapes=[pltpu.VMEM((IN_TILE // LANES, tm, LANES), F32)],
        compiler_params=params,
        name="in_proj_attn",
    )(xb, w_in)
    return hg, qkv


def _hgrn_masks():
    t = np.arange(CHUNK)[:, None]
    u = np.arange(CHUNK)[None, :]
    masks = [(t == u)]
    for c in HGRN_LEVELS:
        same = (t // (2 * c)) == (u // (2 * c))
        masks.append(same & ((t % (2 * c)) >= c) & ((u % (2 * c)) < c))
    return np.stack(masks, axis=0).astype(np.float32)


def _rows(x, r, n):
    return jnp.broadcast_to(x[r:r + 1], (n, x.shape[-1]))


def _chunk_cumsum(x, reverse):
    sub = lax.broadcasted_iota(jnp.int32, x.shape, 0) % SUBLANES
    step = 1
    while step < SUBLANES:
        if reverse:
            x = x + jnp.where(sub < SUBLANES - step, pltpu.roll(x, CHUNK - step, axis=0), 0.0)
        else:
            x = x + jnp.where(sub >= step, pltpu.roll(x, step, axis=0), 0.0)
        step *= 2
    tiles = [x[j:j + SUBLANES] for j in range(0, CHUNK, SUBLANES)]
    if reverse:
        tiles = tiles[::-1]
    edge = 0 if reverse else SUBLANES - 1
    for j in range(1, len(tiles)):
        tiles[j] = tiles[j] + _rows(tiles[j - 1], edge, SUBLANES)
    if reverse:
        tiles = tiles[::-1]
    return jnp.concatenate(tiles, axis=0)


def _query_side(c, reverse):
    t = np.arange(CHUNK)
    return ((t % (2 * c)) >= c) != reverse


def _hgrn_signs(reverse):
    log2e = np.float32(np.log2(np.e))
    sign = np.stack([np.where(_query_side(c, reverse), log2e, -log2e) for c in HGRN_LEVELS])
    return np.broadcast_to(sign[:, :, None], (N_LEVELS, CHUNK, HEAD_DIM)).astype(np.float32)


def _level_boundary(g, c, reverse):
    first = c if reverse else c - 1
    if 2 * c >= SUBLANES:
        return jnp.concatenate([_rows(g, b, 2 * c) for b in range(first, CHUNK, 2 * c)], axis=0)
    sub = lax.broadcasted_iota(jnp.int32, (SUBLANES, g.shape[-1]), 0)
    return jnp.concatenate(
        [jnp.where(sub < 2 * c, _rows(g, j + first, SUBLANES), _rows(g, j + 2 * c + first, SUBLANES))
         for j in range(0, CHUNK, SUBLANES)], axis=0)


def _hgrn_chunk(q, z, v, lb, masks, signs, state_ref, reverse):
    f = lb + (1.0 - lb) * jax.nn.sigmoid(z)
    k = 1.0 - f
    g_incl = _chunk_cumsum(jnp.log(f), reverse)
    last_row = 0 if reverse else CHUNK - 1
    decay_in = jnp.exp(g_incl)
    decay_out = jnp.exp(_rows(g_incl, last_row, CHUNK) - g_incl)
    decay_all = jnp.exp(g_incl[last_row:last_row + 1])

    qb, kb, vb = q.astype(BF16), k.astype(BF16), v.astype(BF16)
    diag = _dot_nt(qb, kb) * masks[0]
    a = [diag[j:j + SUBLANES] for j in range(0, CHUNK, SUBLANES)]
    sub = lax.broadcasted_iota(jnp.int32, q.shape, 0)
    for lv, c in enumerate(HGRN_LEVELS):
        mask = masks[lv + 1]
        query_side = _query_side(c, reverse)
        if c < SUBLANES:
            rows_q = ((sub % (2 * c)) >= c) != reverse
            if c == 1:
                mixed = jnp.where(rows_q, q * f, k)
            else:
                decay = jnp.exp2((g_incl - _level_boundary(g_incl, c, reverse)) * signs[lv])
                mixed = jnp.where(rows_q, q, k) * decay
            mixed = mixed.astype(BF16)
            prod = _dot_nt(mixed, mixed) * mask
            a = [a[j] + prod[j * SUBLANES:(j + 1) * SUBLANES] for j in range(len(a))]
            continue
        decay = jnp.exp2((g_incl - _level_boundary(g_incl, c, reverse)) * signs[lv])
        mixed = jnp.concatenate([(q if query_side[r] else k)[r:r + c] for r in range(0, CHUNK, c)], axis=0) * decay
        q_rows = [r for r in range(0, CHUNK, SUBLANES) if query_side[r]]
        lhs = jnp.concatenate([mixed[r:r + SUBLANES] for r in q_rows], axis=0).astype(BF16)
        prod = _dot_nt(lhs, mixed.astype(BF16))
        for i, r in enumerate(q_rows):
            j = r // SUBLANES
            a[j] = a[j] + prod[i * SUBLANES:(i + 1) * SUBLANES] * mask[r:r + SUBLANES]
    a = jnp.concatenate(a, axis=0)
    state = state_ref[...]
    out = jnp.dot(a.astype(BF16), vb, preferred_element_type=F32)
    out = out + _dot_nt((q * decay_in).astype(BF16), state.astype(BF16))
    state_ref[...] = state * decay_all + _dot_tn(vb, (k * decay_out).astype(BF16))
    return out


def _hgrn_kernel(qf_ref, zf_ref, vf_ref, gf_ref, qb_ref, zb_ref, vb_ref, gb_ref, lbf_ref, lbb_ref, ng_ref,
                 mf_ref, mb_ref, sgf_ref, sgb_ref, o_ref, part_ref, stf_ref, stb_ref, *, n_tiles, tile):
    s = pl.program_id(2)
    n_heads = qf_ref.shape[0]
    chunks = tile // CHUNK

    @pl.when(s == 0)
    def _():
        stf_ref[...] = jnp.zeros_like(stf_ref)
        stb_ref[...] = jnp.zeros_like(stb_ref)

    def lower_bound(ref, h):
        p = ref[h]
        p = jnp.exp(p - jnp.max(p, axis=0, keepdims=True))
        return p[0:1] / jnp.sum(p, axis=0, keepdims=True)

    def scan(finish):
        def scan_step(c, carry):
            lf = pl.multiple_of(c * CHUNK, CHUNK)
            lb = pl.multiple_of((chunks - 1 - c) * CHUNK, CHUNK)
            rows_f = pl.ds(lf, CHUNK)
            rows_b = pl.ds(lb, CHUNK)
            out_f = pl.ds(pl.multiple_of(s * tile + lf, CHUNK), CHUNK)
            out_b = pl.ds(pl.multiple_of((n_tiles - 1 - s) * tile + lb, CHUNK), CHUNK)
            res = []
            for h in range(n_heads):
                res.append(_hgrn_chunk(qf_ref[h, rows_f, :], zf_ref[h, rows_f, :], vf_ref[h, rows_f, :],
                                       lower_bound(lbf_ref, h), mf_ref[...], sgf_ref[...], stf_ref.at[h], False))
                res.append(_hgrn_chunk(qb_ref[h, rows_b, :], zb_ref[h, rows_b, :], vb_ref[h, rows_b, :],
                                       lower_bound(lbb_ref, h), mb_ref[...], sgb_ref[...], stb_ref.at[h], True))
            for h in range(n_heads):
                for o, rows, out_rows, g_ref in ((res[2 * h], rows_f, out_f, gf_ref),
                                                 (res[2 * h + 1], rows_b, out_b, gb_ref)):
                    if not finish:
                        part_ref[h, out_rows, :] = o
                        continue
                    o = o + part_ref[h, out_rows, :]
                    o = o * lax.rsqrt(jnp.mean(o * o, axis=-1, keepdims=True) + RMS_EPS)
                    gate = g_ref[h, rows, :]
                    o = o * ng_ref[h] * (gate * jax.nn.sigmoid(gate))
                    o_ref[out_rows, h * HEAD_DIM:(h + 1) * HEAD_DIM] = o.astype(BF16)
            return carry

        lax.fori_loop(0, chunks, scan_step, 0, unroll=HGRN_UNROLL)

    @pl.when(s < n_tiles // 2)
    def _():
        scan(finish=False)

    @pl.when(s >= n_tiles // 2)
    def _():
        scan(finish=True)


HGRN_HEADS_PER_STEP = 4
HGRN_TILE = 512
HGRN_UNROLL = 4


def _hgrn(hg, lb_fwd, lb_bwd, norm_g, batch, seq):
    hg5 = hg.reshape(N_HGRN_PARTS, N_HEADS_HGRN, batch, seq, HEAD_DIM)
    masks = _hgrn_masks()
    masks_f = jnp.asarray(masks, F32)
    masks_b = jnp.asarray(masks[:, ::-1, ::-1], F32)
    hps, tile = HGRN_HEADS_PER_STEP, HGRN_TILE
    n_tiles = seq // tile
    assert seq % tile == 0 and n_tiles % 2 == 0 and N_HEADS_HGRN % hps == 0

    def part(p, backward):
        def index(b, hg_, s):
            return (p, hg_, b, (n_tiles - 1 - s) if backward else s, 0)
        return pl.BlockSpec((None, hps, None, tile, HEAD_DIM), index)

    def per_head(rows):
        return pl.BlockSpec((hps, rows, HEAD_DIM), lambda b, hg_, s: (hg_, 0, 0))

    def lb_per_head(lb):
        return lb.reshape(DEPTH + 1, N_HEADS_HGRN, HEAD_DIM).transpose(1, 0, 2)

    const3 = lambda b, hg_, s: (0, 0, 0)
    q, zf, zb, v, g = range(N_HGRN_PARTS)
    return pl.pallas_call(
        functools.partial(_hgrn_kernel, n_tiles=n_tiles, tile=tile),
        grid=(batch, N_HEADS_HGRN // hps, n_tiles),
        in_specs=[
            part(q, False), part(zf, False), part(v, False), part(g, False),
            part(q, True), part(zb, True), part(v, True), part(g, True),
            per_head(DEPTH + 1), per_head(DEPTH + 1), per_head(1),
            pl.BlockSpec((N_LEVELS + 1, CHUNK, CHUNK), const3),
            pl.BlockSpec((N_LEVELS + 1, CHUNK, CHUNK), const3),
            pl.BlockSpec((N_LEVELS, CHUNK, HEAD_DIM), const3),
            pl.BlockSpec((N_LEVELS, CHUNK, HEAD_DIM), const3),
        ],
        out_specs=pl.BlockSpec((None, seq, hps * HEAD_DIM), lambda b, hg_, s: (b, 0, hg_)),
        out_shape=jax.ShapeDtypeStruct((batch, seq, HGRN_WIDTH), BF16),
        scratch_shapes=[
            pltpu.VMEM((hps, seq, HEAD_DIM), F32),
            pltpu.VMEM((hps, HEAD_DIM, HEAD_DIM), F32), pltpu.VMEM((hps, HEAD_DIM, HEAD_DIM), F32),
        ],
        compiler_params=pltpu.CompilerParams(
            dimension_semantics=("parallel", "parallel", "arbitrary"), vmem_limit_bytes=VMEM_LIMIT_BYTES),
        name="hgrn",
    )(hg5, hg5, hg5, hg5, hg5, hg5, hg5, hg5,
      lb_per_head(lb_fwd), lb_per_head(lb_bwd), norm_g.reshape(N_HEADS_HGRN, 1, HEAD_DIM),
      masks_f, masks_b, jnp.asarray(_hgrn_signs(False)), jnp.asarray(_hgrn_signs(True)))


def _attn_block_types(n_rows):
    row = np.arange(ATTN_BLOCK)

    def flags(n):
        i = n * ATTN_BLOCK + row
        reach = [ATTN_HALF * d // ATTN_CLASSES for d in ATTN_DILATIONS]
        return np.stack([i < r for r in reach] + [i >= n_rows - r for r in reach]).tobytes()

    starts = [0]
    for n in range(1, n_rows // ATTN_BLOCK):
        if flags(n) != flags(n - 1):
            starts.append(n)
    return tuple(starts)


def _attn_bias(delta, i_query, slope, n_rows):
    mult = jnp.zeros(delta.shape, F32)
    for d in ATTN_DILATIONS:
        reach = ATTN_HALF * d // ATTN_CLASSES
        lo = jnp.where(i_query < reach, 0, -ATTN_HALF * d)
        hi = jnp.where(i_query >= n_rows - reach, (ATTN_HALF - 1) * d, ATTN_HALF * d)
        hit = (delta >= lo) & (delta <= hi)
        if d > 1:
            hit = hit & (lax.rem(delta, d) == 0)
        mult = mult + hit.astype(F32)
    dist = jnp.abs(delta).astype(F32)
    return jnp.where(mult > 0.0, jnp.log(jnp.maximum(mult, 1.0)) - slope * dist, NEG_INF)


def _lane_reduce(parts, combine, reduce):
    folded = {}
    for part in parts:
        for j in range(0, part.shape[-1], LANES):
            tile = part[:, j:j + LANES]
            w = tile.shape[-1]
            folded[w] = tile if w not in folded else combine(folded[w], tile)
    out = None
    for tile in folded.values():
        r = reduce(tile, axis=-1, keepdims=True)
        out = r if out is None else combine(out, r)
    return out


def _attn_kernel(slope_ref, q_ref, k_ref, v_ref, o_ref, kpad_ref, vpad_ref, bias_same_ref, bias_other_ref,
                 *, n_rows, type_starts):
    b = pl.program_id(1)

    @pl.when(b == 0)
    def _():
        slope = slope_ref[...][:, 0:1]
        row = lax.broadcasted_iota(jnp.int32, (ATTN_BLOCK, ATTN_SAME), 0)
        col = lax.broadcasted_iota(jnp.int32, (ATTN_BLOCK, ATTN_SAME), 1)
        for ty, first_block in enumerate(type_starts):
            bias_same_ref[ty] = _attn_bias((col - ATTN_PAD - row) * ATTN_CLASSES,
                                           first_block * ATTN_BLOCK + row, slope, n_rows)
        row = lax.broadcasted_iota(jnp.int32, (ATTN_BLOCK, ATTN_OTHER), 0)
        col = lax.broadcasted_iota(jnp.int32, (ATTN_BLOCK, ATTN_OTHER), 1)
        for query_class in range(ATTN_CLASSES):
            for m in range(1, ATTN_CLASSES):
                delta = (col - ATTN_OTHER_LEAD - row) * ATTN_CLASSES + ((query_class + m) % ATTN_CLASSES - query_class)
                for ty, first_block in enumerate(type_starts):
                    bias_other_ref[query_class, ty, m - 1] = _attn_bias(
                        delta, first_block * ATTN_BLOCK + row, slope, n_rows)

    zeros = jnp.zeros((ATTN_PAD, HEAD_DIM), BF16)
    for c in range(ATTN_CLASSES):
        for pad_ref, src_ref in ((kpad_ref, k_ref), (vpad_ref, v_ref)):
            pad_ref[c, 0:ATTN_PAD, :] = zeros
            pad_ref[c, ATTN_PAD:ATTN_PAD + n_rows, :] = src_ref[c]
            pad_ref[c, ATTN_PAD + n_rows:ATTN_PAD + n_rows + ATTN_PAD, :] = zeros

    scale = HEAD_DIM ** -0.5

    def query_class(rq, carry):
        def block(n, carry):
            i0 = pl.multiple_of(n * ATTN_BLOCK, ATTN_BLOCK)
            ty = jnp.int32(0)
            for first_block in type_starts[1:]:
                ty = ty + jnp.asarray(n >= first_block, jnp.int32)
            qb = q_ref[rq, pl.ds(i0, ATTN_BLOCK), :]
            scores, values = [], []
            s = _dot_nt(qb, kpad_ref[rq, pl.ds(i0, ATTN_SAME), :]) * scale
            scores.append(s + bias_same_ref[ty])
            values.append(vpad_ref[rq, pl.ds(i0, ATTN_SAME), :])
            start = pl.multiple_of(i0 + (ATTN_PAD - ATTN_OTHER_LEAD), ATTN_OTHER_LEAD)
            for m in range(1, ATTN_CLASSES):
                rk = lax.rem(rq + m, ATTN_CLASSES)
                s = _dot_nt(qb, kpad_ref[rk, pl.ds(start, ATTN_OTHER), :]) * scale
                scores.append(s + bias_other_ref[rq, ty, m - 1])
                values.append(vpad_ref[rk, pl.ds(start, ATTN_OTHER), :])
            mx = _lane_reduce(scores, jnp.maximum, jnp.max)
            probs = [jnp.exp(s - mx) for s in scores]
            den = _lane_reduce(probs, jnp.add, jnp.sum)
            acc = jnp.zeros((ATTN_BLOCK, HEAD_DIM), F32)
            for p, v in zip(probs, values):
                acc = acc + jnp.dot(p.astype(BF16), v, preferred_element_type=F32)
            o_ref[rq, pl.ds(i0, ATTN_BLOCK), :] = (acc / den).astype(BF16)
            return carry

        return lax.fori_loop(0, n_rows // ATTN_BLOCK, block, carry, unroll=ATTN_UNROLL)

    lax.fori_loop(0, ATTN_CLASSES, query_class, 0)


def _attn(qkv, batch, seq):
    n_rows = seq // ATTN_CLASSES
    assert seq % (ATTN_HALF * max(ATTN_DILATIONS)) == 0 and seq >= 2 * ATTN_HALF * max(ATTN_DILATIONS)
    assert n_rows % (ATTN_BLOCK * ATTN_UNROLL) == 0
    type_starts = _attn_block_types(n_rows)
    n_types = len(type_starts)
    qkv = qkv.reshape(N_ATTN_PARTS, N_HEADS_ATTN, ATTN_CLASSES, batch, n_rows, HEAD_DIM)
    slopes = 2.0 ** (-8.0 * (jnp.arange(N_HEADS_ATTN, dtype=F32) + 1.0) / N_HEADS_ATTN)
    slopes = jnp.broadcast_to(slopes[:, None, None], (N_HEADS_ATTN, 1, HEAD_DIM))

    def part(p):
        return pl.BlockSpec((None, None, ATTN_CLASSES, None, n_rows, HEAD_DIM), lambda h, b: (p, h, 0, b, 0, 0))

    padded = n_rows + 2 * ATTN_PAD
    out = pl.pallas_call(
        functools.partial(_attn_kernel, n_rows=n_rows, type_starts=type_starts),
        grid=(N_HEADS_ATTN, batch),
        in_specs=[pl.BlockSpec((None, 1, HEAD_DIM), lambda h, b: (h, 0, 0)), part(0), part(1), part(2)],
        out_specs=pl.BlockSpec((None, ATTN_CLASSES, None, n_rows, HEAD_DIM), lambda h, b: (h, 0, b, 0, 0)),
        out_shape=jax.ShapeDtypeStruct((N_HEADS_ATTN, ATTN_CLASSES, batch, n_rows, HEAD_DIM), BF16),
        scratch_shapes=[
            pltpu.VMEM((ATTN_CLASSES, padded, HEAD_DIM), BF16),
            pltpu.VMEM((ATTN_CLASSES, padded, HEAD_DIM), BF16),
            pltpu.VMEM((n_types, ATTN_BLOCK, ATTN_SAME), F32),
            pltpu.VMEM((ATTN_CLASSES, n_types, ATTN_CLASSES - 1, ATTN_BLOCK, ATTN_OTHER), F32),
        ],
        compiler_params=pltpu.CompilerParams(
            dimension_semantics=("arbitrary", "arbitrary"), vmem_limit_bytes=VMEM_LIMIT_BYTES),
        name="attn",
    )(slopes, qkv, qkv, qkv)
    return out.reshape(ATTN_SLABS, batch * n_rows, HEAD_DIM)


def _out_proj_kernel(hg_ref, at_ref, w_ref, x_ref, g_ref, b_ref, o_ref, y_ref):
    _store_lane_tiles(y_ref, ALPHA * x_ref[...]
                      + jnp.dot(hg_ref[...], w_ref[0:HGRN_WIDTH, :], preferred_element_type=F32))
    rows = at_ref.shape[1]
    n_tiles = y_ref.shape[0]
    for c in range(ATTN_CLASSES):
        at_c = jnp.concatenate([at_ref[h * ATTN_CLASSES + c] for h in range(N_HEADS_ATTN)], axis=-1)
        mix = jnp.dot(at_c, w_ref[HGRN_WIDTH:HGRN_WIDTH + ATTN_WIDTH, :], preferred_element_type=F32)
        cls_rows = pl.ds(c, rows, stride=ATTN_CLASSES)
        y = jnp.concatenate([y_ref[j, cls_rows, :] for j in range(n_tiles)], axis=-1) + mix
        y = _layer_norm(y, g_ref[...], b_ref[...])
        for j in range(n_tiles):
            y_ref[j, cls_rows, :] = y[:, j * LANES:(j + 1) * LANES]
    o_ref[...] = jnp.concatenate([y_ref[j] for j in range(n_tiles)], axis=-1)


def _out_proj(hg, at4, w_out, x, g, b, *, tm=512):
    t, d = x.shape
    tok = lambda i: (i, 0)
    const = lambda i: (0, 0)
    return pl.pallas_call(
        _out_proj_kernel,
        grid=(t // tm,),
        in_specs=[
            pl.BlockSpec((tm, HGRN_WIDTH), tok),
            pl.BlockSpec((ATTN_SLABS, tm // ATTN_CLASSES, HEAD_DIM), lambda i: (0, i, 0)),
            pl.BlockSpec((HGRN_WIDTH + ATTN_WIDTH, d), const),
            pl.BlockSpec((tm, d), tok),
            pl.BlockSpec((1, d), const),
            pl.BlockSpec((1, d), const),
        ],
        out_specs=pl.BlockSpec((tm, d), tok),
        out_shape=jax.ShapeDtypeStruct((t, d), F32),
        scratch_shapes=[pltpu.VMEM((d // LANES, tm, LANES), F32)],
        compiler_params=pltpu.CompilerParams(
            dimension_semantics=("parallel",), vmem_limit_bytes=VMEM_LIMIT_BYTES),
        name="out_proj",
    )(hg, at4, w_out, x, g, b)


def _encoder_layer(x, w, layer):
    batch, seq, d = x.shape
    h = x.reshape(batch * seq, d)
    h, hb = _ffn(h, w["ffn1_w_gate"][layer], w["ffn1_w_up"][layer], w["ffn1_w_down"][layer],
                 w["ln1_g"][layer:layer + 1], w["ln1_b"][layer:layer + 1], emit_bf16=True)
    hg, qkv = _in_proj(hb, w["w_in"][layer])
    hgrn_out = _hgrn(hg, w["hgrn_lb_fwd"], w["hgrn_lb_bwd"], w["hgrn_norm_g"][layer], batch, seq)
    attn_out = _attn(qkv, batch, seq)
    h = _out_proj(hgrn_out.reshape(batch * seq, HGRN_WIDTH), attn_out, w["w_out"][layer], h,
                  w["ln2_g"][layer:layer + 1], w["ln2_b"][layer:layer + 1])
    h, = _ffn(h, w["ffn2_w_gate"][layer], w["ffn2_w_up"][layer], w["ffn2_w_down"][layer],
              w["ln3_g"][layer:layer + 1], w["ln3_b"][layer:layer + 1], emit_bf16=False)
    return h.reshape(batch, seq, d)


_MATMUL_WEIGHTS = ("ffn1_w_gate", "ffn1_w_up", "ffn1_w_down", "w_in", "w_out",
                   "ffn2_w_gate", "ffn2_w_up", "ffn2_w_down")


def kernel(x_prompt, x_sample, ln1_g, ln1_b, ffn1_w_gate, ffn1_w_up, ffn1_w_down, ln2_g, ln2_b, w_in,
           hgrn_lb_fwd, hgrn_lb_bwd, hgrn_norm_g, w_out, ln3_g, ln3_b, ffn2_w_gate, ffn2_w_up, ffn2_w_down):
    w = dict(ln1_g=ln1_g, ln1_b=ln1_b, ffn1_w_gate=ffn1_w_gate, ffn1_w_up=ffn1_w_up, ffn1_w_down=ffn1_w_down,
             ln2_g=ln2_g, ln2_b=ln2_b, w_in=w_in, hgrn_lb_fwd=hgrn_lb_fwd, hgrn_lb_bwd=hgrn_lb_bwd,
             hgrn_norm_g=hgrn_norm_g, w_out=w_out, ln3_g=ln3_g, ln3_b=ln3_b,
             ffn2_w_gate=ffn2_w_gate, ffn2_w_up=ffn2_w_up, ffn2_w_down=ffn2_w_down)
    for name in _MATMUL_WEIGHTS:
        w[name] = w[name].astype(BF16)
    outs = []
    for x in (x_prompt, x_sample):
        for layer in range(DEPTH):
            x = _encoder_layer(x, w, layer)
        outs.append(x)
    return tuple(outs)
```

```python
import functools

import numpy as np
import jax
import jax.numpy as jnp
from jax import lax
from jax.experimental import pallas as pl
from jax.experimental.pallas import tpu as pltpu

F32 = jnp.float32
BF16 = jnp.bfloat16

D_MODEL = 2048
D_FF = 5632
DEPTH = 1
HEAD_DIM = 128
N_HEADS_HGRN = 8
N_HEADS_ATTN = 8
HGRN_WIDTH = N_HEADS_HGRN * HEAD_DIM
ATTN_WIDTH = N_HEADS_ATTN * HEAD_DIM
IN_WIDTH = 5 * HGRN_WIDTH + 3 * ATTN_WIDTH
ALPHA = (2 * DEPTH) ** 0.25
LN_EPS = 1e-5
RMS_EPS = 1e-6
NEG_INF = -1e30

CHUNK = 64
HGRN_LEVELS = (1, 2, 4, 8, 16, 32)
N_LEVELS = len(HGRN_LEVELS)

ATTN_HALF = 64
ATTN_DILATIONS = (1, 4, 16)
ATTN_CLASSES = 4
ATTN_BLOCK = 128
ATTN_PAD = 256
ATTN_SAME = ATTN_BLOCK + 2 * ATTN_PAD
ATTN_OTHER_LEAD = 32
ATTN_OTHER = ATTN_BLOCK + 2 * ATTN_OTHER_LEAD
ATTN_UNROLL = 4
LANES = 128
SUBLANES = 8

VMEM_LIMIT_BYTES = 56 * 1024 * 1024
FFN_VMEM_LIMIT_BYTES = 60 * 1024 * 1024


def _layer_norm(y, g, b):
    mu = jnp.mean(y, axis=-1, keepdims=True)
    yc = y - mu
    var = jnp.mean(yc * yc, axis=-1, keepdims=True)
    return yc * lax.rsqrt(var + LN_EPS) * g + b


def _dot_nt(a, b):
    return lax.dot_general(a, b, (((1,), (1,)), ((), ())), preferred_element_type=F32)


def _dot_tn(a, b):
    return lax.dot_general(a, b, (((0,), (0,)), ((), ())), preferred_element_type=F32)


def _ffn_kernel(x_ref, wg_ref, wu_ref, wd_ref, g_ref, b_ref, *refs, n_ff_tiles, emit_bf16):
    if emit_bf16:
        o_ref, ob_ref, xb_ref = refs
    else:
        o_ref, xb_ref = refs
    j = pl.program_id(1)

    @pl.when(j == 0)
    def _():
        xb_ref[...] = x_ref[...].astype(BF16)
        o_ref[...] = jnp.zeros_like(o_ref)

    xb = xb_ref[...]
    gate = jnp.dot(xb, wg_ref[...], preferred_element_type=F32)
    up = jnp.dot(xb, wu_ref[...], preferred_element_type=F32)
    hidden = (gate * jax.nn.sigmoid(gate) * up).astype(BF16)
    o_ref[...] += jnp.dot(hidden, wd_ref[...], preferred_element_type=F32)

    @pl.when(j == n_ff_tiles - 1)
    def _():
        y = _layer_norm(ALPHA * x_ref[...] + 0.5 * o_ref[...], g_ref[...], b_ref[...])
        o_ref[...] = y
        if emit_bf16:
            ob_ref[...] = y.astype(BF16)


def _ffn(x, wg, wu, wd, g, b, *, emit_bf16, tm=1024, tf=256):
    t, d = x.shape
    n_ff_tiles = D_FF // tf
    tok = lambda i, j: (i, 0)
    out_specs = [pl.BlockSpec((tm, d), tok)]
    out_shape = [jax.ShapeDtypeStruct((t, d), F32)]
    if emit_bf16:
        out_specs.append(pl.BlockSpec((tm, d), tok))
        out_shape.append(jax.ShapeDtypeStruct((t, d), BF16))
    return pl.pallas_call(
        functools.partial(_ffn_kernel, n_ff_tiles=n_ff_tiles, emit_bf16=emit_bf16),
        grid=(t // tm, n_ff_tiles),
        in_specs=[
            pl.BlockSpec((tm, d), tok),
            pl.BlockSpec((d, tf), lambda i, j: (0, j)),
            pl.BlockSpec((d, tf), lambda i, j: (0, j)),
            pl.BlockSpec((tf, d), lambda i, j: (j, 0)),
            pl.BlockSpec((1, d), lambda i, j: (0, 0)),
            pl.BlockSpec((1, d), lambda i, j: (0, 0)),
        ],
        out_specs=out_specs,
        out_shape=out_shape,
        scratch_shapes=[pltpu.VMEM((tm, d), BF16)],
        compiler_params=pltpu.CompilerParams(
            dimension_semantics=("parallel", "arbitrary"), vmem_limit_bytes=FFN_VMEM_LIMIT_BYTES),
        name="ffn",
    )(x, wg, wu, wd, g, b)


N_HGRN_PARTS = 5
N_ATTN_PARTS = 3
ATTN_SLABS = N_HEADS_ATTN * ATTN_CLASSES
IN_TILE = 1024


def _proj_heads_kernel(x_ref, w_ref, o_ref):
    r = jnp.dot(x_ref[...], w_ref[...], preferred_element_type=F32)
    for c in range(N_HEADS_HGRN):
        o_ref[c] = r[:, c * HEAD_DIM:(c + 1) * HEAD_DIM]


def _store_lane_tiles(ref, x):
    for j in range(ref.shape[0]):
        ref[j] = x[:, j * LANES:(j + 1) * LANES]


def _proj_classes_kernel(x_ref, w_ref, o_ref, r_ref):
    _store_lane_tiles(r_ref, jnp.dot(x_ref[...], w_ref[...], preferred_element_type=F32))
    rows = o_ref.shape[1]
    for c in range(ATTN_CLASSES):
        cls_rows = pl.ds(c, rows, stride=ATTN_CLASSES)
        for h in range(N_HEADS_ATTN):
            o_ref[h * ATTN_CLASSES + c] = r_ref[h, cls_rows, :].astype(BF16)


def _in_proj(xb, w_in, *, tm=1024):
    t, d = xb.shape
    params = pltpu.CompilerParams(dimension_semantics=("parallel", "arbitrary"), vmem_limit_bytes=VMEM_LIMIT_BYTES)
    hg = pl.pallas_call(
        _proj_heads_kernel,
        grid=(t // tm, N_HGRN_PARTS),
        in_specs=[pl.BlockSpec((tm, d), lambda i, j: (i, 0)), pl.BlockSpec((d, IN_TILE), lambda i, j: (0, j))],
        out_specs=pl.BlockSpec((N_HEADS_HGRN, tm, HEAD_DIM), lambda i, j: (j, i, 0)),
        out_shape=jax.ShapeDtypeStruct((N_HGRN_PARTS * N_HEADS_HGRN, t, HEAD_DIM), F32),
        compiler_params=params,
        name="in_proj_hgrn",
    )(xb, w_in)
    qkv = pl.pallas_call(
        _proj_classes_kernel,
        grid=(t // tm, N_ATTN_PARTS),
        in_specs=[pl.BlockSpec((tm, d), lambda i, j: (i, 0)),
                  pl.BlockSpec((d, IN_TILE), lambda i, j: (0, N_HGRN_PARTS + j))],
        out_specs=pl.BlockSpec((None, ATTN_SLABS, tm // ATTN_CLASSES, HEAD_DIM), lambda i, j: (j, 0, i, 0)),
        out_shape=jax.ShapeDtypeStruct((N_ATTN_PARTS, ATTN_SLABS, t // ATTN_CLASSES, HEAD_DIM), BF16),
        scratch_shapes=[pltpu.VMEM((IN_TILE // LANES, tm, LANES), F32)],
        compiler_params=params,
        name="in_proj_attn",
    )(xb, w_in)
    return hg, qkv


def _hgrn_masks():
    t = np.arange(CHUNK)[:, None]
    u = np.arange(CHUNK)[None, :]
    masks = [(t == u)]
    for c in HGRN_LEVELS:
        same = (t // (2 * c)) == (u // (2 * c))
        masks.append(same & ((t % (2 * c)) >= c) & ((u % (2 * c)) < c))
    return np.stack(masks, axis=0).astype(np.float32)


def _rows(x, r, n):
    return jnp.broadcast_to(x[r:r + 1], (n, x.shape[-1]))


def _chunk_cumsum(x, reverse):
    sub = lax.broadcasted_iota(jnp.int32, x.shape, 0) % SUBLANES
    step = 1
    while step < SUBLANES:
        if reverse:
            x = x + jnp.where(sub < SUBLANES - step, pltpu.roll(x, CHUNK - step, axis=0), 0.0)
        else:
            x = x + jnp.where(sub >= step, pltpu.roll(x, step, axis=0), 0.0)
        step *= 2
    tiles = [x[j:j + SUBLANES] for j in range(0, CHUNK, SUBLANES)]
    if reverse:
        tiles = tiles[::-1]
    edge = 0 if reverse else SUBLANES - 1
    for j in range(1, len(tiles)):
        tiles[j] = tiles[j] + _rows(tiles[j - 1], edge, SUBLANES)
    if reverse:
        tiles = tiles[::-1]
    return jnp.concatenate(tiles, axis=0)


def _query_side(c, reverse):
    t = np.arange(CHUNK)
    return ((t % (2 * c)) >= c) != reverse


def _hgrn_signs(reverse):
    log2e = np.float32(np.log2(np.e))
    sign = np.stack([np.where(_query_side(c, reverse), log2e, -log2e) for c in HGRN_LEVELS])
    return np.broadcast_to(sign[:, :, None], (N_LEVELS, CHUNK, HEAD_DIM)).astype(np.float32)


def _level_boundary(g, c, reverse):
    first = c if reverse else c - 1
    if 2 * c >= SUBLANES:
        return jnp.concatenate([_rows(g, b, 2 * c) for b in range(first, CHUNK, 2 * c)], axis=0)
    sub = lax.broadcasted_iota(jnp.int32, (SUBLANES, g.shape[-1]), 0)
    return jnp.concatenate(
        [jnp.where(sub < 2 * c, _rows(g, j + first, SUBLANES), _rows(g, j + 2 * c + first, SUBLANES))
         for j in range(0, CHUNK, SUBLANES)], axis=0)


def _hgrn_chunk(q, z, v, lb, masks, signs, state_ref, reverse):
    f = lb + (1.0 - lb) * jax.nn.sigmoid(z)
    k = 1.0 - f
    g_incl = _chunk_cumsum(jnp.log(f), reverse)
    last_row = 0 if reverse else CHUNK - 1
    decay_in = jnp.exp(g_incl)
    decay_out = jnp.exp(_rows(g_incl, last_row, CHUNK) - g_incl)
    decay_all = jnp.exp(g_incl[last_row:last_row + 1])

    qb, kb, vb = q.astype(BF16), k.astype(BF16), v.astype(BF16)
    diag = _dot_nt(qb, kb) * masks[0]
    a = [diag[j:j + SUBLANES] for j in range(0, CHUNK, SUBLANES)]
    sub = lax.broadcasted_iota(jnp.int32, q.shape, 0)
    for lv, c in enumerate(HGRN_LEVELS):
        mask = masks[lv + 1]
        query_side = _query_side(c, reverse)
        if c < SUBLANES:
            rows_q = ((sub % (2 * c)) >= c) != reverse
            if c == 1:
                mixed = jnp.where(rows_q, q * f, k)
            else:
                decay = jnp.exp2((g_incl - _level_boundary(g_incl, c, reverse)) * signs[lv])
                mixed = jnp.where(rows_q, q, k) * decay
            mixed = mixed.astype(BF16)
            prod = _dot_nt(mixed, mixed) * mask
            a = [a[j] + prod[j * SUBLANES:(j + 1) * SUBLANES] for j in range(len(a))]
            continue
        decay = jnp.exp2((g_incl - _level_boundary(g_incl, c, reverse)) * signs[lv])
        mixed = jnp.concatenate([(q if query_side[r] else k)[r:r + c] for r in range(0, CHUNK, c)], axis=0) * decay
        q_rows = [r for r in range(0, CHUNK, SUBLANES) if query_side[r]]
        lhs = jnp.concatenate([mixed[r:r + SUBLANES] for r in q_rows], axis=0).astype(BF16)
        prod = _dot_nt(lhs, mixed.astype(BF16))
        for i, r in enumerate(q_rows):
            j = r // SUBLANES
            a[j] = a[j] + prod[i * SUBLANES:(i + 1) * SUBLANES] * mask[r:r + SUBLANES]
    a = jnp.concatenate(a, axis=0)
    state = state_ref[...]
    out = jnp.dot(a.astype(BF16), vb, preferred_element_type=F32)
    out = out + _dot_nt((q * decay_in).astype(BF16), state.astype(BF16))
    state_ref[...] = state * decay_all + _dot_tn(vb, (k * decay_out).astype(BF16))
    return out


def _hgrn_kernel(qf_ref, zf_ref, vf_ref, gf_ref, qb_ref, zb_ref, vb_ref, gb_ref, lbf_ref, lbb_ref, ng_ref,
                 mf_ref, mb_ref, sgf_ref, sgb_ref, o_ref, part_ref, stf_ref, stb_ref, *, n_tiles, tile):
    s = pl.program_id(2)
    n_heads = qf_ref.shape[0]
    chunks = tile // CHUNK

    @pl.when(s == 0)
    def _():
        stf_ref[...] = jnp.zeros_like(stf_ref)
        stb_ref[...] = jnp.zeros_like(stb_ref)

    def lower_bound(ref, h):
        p = ref[h]
        p = jnp.exp(p - jnp.max(p, axis=0, keepdims=True))
        return p[0:1] / jnp.sum(p, axis=0, keepdims=True)

    def scan(finish):
        def scan_step(c, carry):
            lf = pl.multiple_of(c * CHUNK, CHUNK)
            lb = pl.multiple_of((chunks - 1 - c) * CHUNK, CHUNK)
            rows_f = pl.ds(lf, CHUNK)
            rows_b = pl.ds(lb, CHUNK)
            out_f = pl.ds(pl.multiple_of(s * tile + lf, CHUNK), CHUNK)
            out_b = pl.ds(pl.multiple_of((n_tiles - 1 - s) * tile + lb, CHUNK), CHUNK)
            res = []
            for h in range(n_heads):
                res.append(_hgrn_chunk(qf_ref[h, rows_f, :], zf_ref[h, rows_f, :], vf_ref[h, rows_f, :],
                                       lower_bound(lbf_ref, h), mf_ref[...], sgf_ref[...], stf_ref.at[h], False))
                res.append(_hgrn_chunk(qb_ref[h, rows_b, :], zb_ref[h, rows_b, :], vb_ref[h, rows_b, :],
                                       lower_bound(lbb_ref, h), mb_ref[...], sgb_ref[...], stb_ref.at[h], True))
            for h in range(n_heads):
                for o, rows, out_rows, g_ref in ((res[2 * h], rows_f, out_f, gf_ref),
                                                 (res[2 * h + 1], rows_b, out_b, gb_ref)):
                    if not finish:
                        part_ref[h, out_rows, :] = o
                        continue
                    o = o + part_ref[h, out_rows, :]
                    o = o * lax.rsqrt(jnp.mean(o * o, axis=-1, keepdims=True) + RMS_EPS)
                    gate = g_ref[h, rows, :]
                    o = o * ng_ref[h] * (gate * jax.nn.sigmoid(gate))
                    o_ref[out_rows, h * HEAD_DIM:(h + 1) * HEAD_DIM] = o.astype(BF16)
            return carry

        lax.fori_loop(0, chunks, scan_step, 0, unroll=HGRN_UNROLL)

    @pl.when(s < n_tiles // 2)
    def _():
        scan(finish=False)

    @pl.when(s >= n_tiles // 2)
    def _():
        scan(finish=True)


HGRN_HEADS_PER_STEP = 4
HGRN_TILE = 512
HGRN_UNROLL = 4


def _hgrn(hg, lb_fwd, lb_bwd, norm_g, batch, seq):
    hg5 = hg.reshape(N_HGRN_PARTS, N_HEADS_HGRN, batch, seq, HEAD_DIM)
    masks = _hgrn_masks()
    masks_f = jnp.asarray(masks, F32)
    masks_b = jnp.asarray(masks[:, ::-1, ::-1], F32)
    hps, tile = HGRN_HEADS_PER_STEP, HGRN_TILE
    n_tiles = seq // tile
    assert seq % tile == 0 and n_tiles % 2 == 0 and N_HEADS_HGRN % hps == 0

    def part(p, backward):
        def index(b, hg_, s):
            return (p, hg_, b, (n_tiles - 1 - s) if backward else s, 0)
        return pl.BlockSpec((None, hps, None, tile, HEAD_DIM), index)

    def per_head(rows):
        return pl.BlockSpec((hps, rows, HEAD_DIM), lambda b, hg_, s: (hg_, 0, 0))

    def lb_per_head(lb):
        return lb.reshape(DEPTH + 1, N_HEADS_HGRN, HEAD_DIM).transpose(1, 0, 2)

    const3 = lambda b, hg_, s: (0, 0, 0)
    q, zf, zb, v, g = range(N_HGRN_PARTS)
    return pl.pallas_call(
        functools.partial(_hgrn_kernel, n_tiles=n_tiles, tile=tile),
        grid=(batch, N_HEADS_HGRN // hps, n_tiles),
        in_specs=[
            part(q, False), part(zf, False), part(v, False), part(g, False),
            part(q, True), part(zb, True), part(v, True), part(g, True),
            per_head(DEPTH + 1), per_head(DEPTH + 1), per_head(1),
            pl.BlockSpec((N_LEVELS + 1, CHUNK, CHUNK), const3),
            pl.BlockSpec((N_LEVELS + 1, CHUNK, CHUNK), const3),
            pl.BlockSpec((N_LEVELS, CHUNK, HEAD_DIM), const3),
            pl.BlockSpec((N_LEVELS, CHUNK, HEAD_DIM), const3),
        ],
        out_specs=pl.BlockSpec((None, seq, hps * HEAD_DIM), lambda b, hg_, s: (b, 0, hg_)),
        out_shape=jax.ShapeDtypeStruct((batch, seq, HGRN_WIDTH), BF16),
        scratch_shapes=[
            pltpu.VMEM((hps, seq, HEAD_DIM), F32),
            pltpu.VMEM((hps, HEAD_DIM, HEAD_DIM), F32), pltpu.VMEM((hps, HEAD_DIM, HEAD_DIM), F32),
        ],
        compiler_params=pltpu.CompilerParams(
            dimension_semantics=("parallel", "parallel", "arbitrary"), vmem_limit_bytes=VMEM_LIMIT_BYTES),
        name="hgrn",
    )(hg5, hg5, hg5, hg5, hg5, hg5, hg5, hg5,
      lb_per_head(lb_fwd), lb_per_head(lb_bwd), norm_g.reshape(N_HEADS_HGRN, 1, HEAD_DIM),
      masks_f, masks_b, jnp.asarray(_hgrn_signs(False)), jnp.asarray(_hgrn_signs(True)))


def _attn_block_types(n_rows):
    row = np.arange(ATTN_BLOCK)

    def flags(n):
        i = n * ATTN_BLOCK + row
        reach = [ATTN_HALF * d // ATTN_CLASSES for d in ATTN_DILATIONS]
        return np.stack([i < r for r in reach] + [i >= n_rows - r for r in reach]).tobytes()

    starts = [0]
    for n in range(1, n_rows // ATTN_BLOCK):
        if flags(n) != flags(n - 1):
            starts.append(n)
    return tuple(starts)


def _attn_bias(delta, i_query, slope, n_rows):
    mult = jnp.zeros(delta.shape, F32)
    for d in ATTN_DILATIONS:
        reach = ATTN_HALF * d // ATTN_CLASSES
        lo = jnp.where(i_query < reach, 0, -ATTN_HALF * d)
        hi = jnp.where(i_query >= n_rows - reach, (ATTN_HALF - 1) * d, ATTN_HALF * d)
        hit = (delta >= lo) & (delta <= hi)
        if d > 1:
            hit = hit & (lax.rem(delta, d) == 0)
        mult = mult + hit.astype(F32)
    dist = jnp.abs(delta).astype(F32)
    return jnp.where(mult > 0.0, jnp.log(jnp.maximum(mult, 1.0)) - slope * dist, NEG_INF)


def _lane_reduce(parts, combine, reduce):
    folded = {}
    for part in parts:
        for j in range(0, part.shape[-1], LANES):
            tile = part[:, j:j + LANES]
            w = tile.shape[-1]
            folded[w] = tile if w not in folded else combine(folded[w], tile)
    out = None
    for tile in folded.values():
        r = reduce(tile, axis=-1, keepdims=True)
        out = r if out is None else combine(out, r)
    return out


def _attn_kernel(slope_ref, q_ref, k_ref, v_ref, o_ref, kpad_ref, vpad_ref, bias_same_ref, bias_other_ref,
                 *, n_rows, type_starts):
    b = pl.program_id(1)

    @pl.when(b == 0)
    def _():
        slope = slope_ref[...][:, 0:1]
        row = lax.broadcasted_iota(jnp.int32, (ATTN_BLOCK, ATTN_SAME), 0)
        col = lax.broadcasted_iota(jnp.int32, (ATTN_BLOCK, ATTN_SAME), 1)
        for ty, first_block in enumerate(type_starts):
            bias_same_ref[ty] = _attn_bias((col - ATTN_PAD - row) * ATTN_CLASSES,
                                           first_block * ATTN_BLOCK + row, slope, n_rows)
        row = lax.broadcasted_iota(jnp.int32, (ATTN_BLOCK, ATTN_OTHER), 0)
        col = lax.broadcasted_iota(jnp.int32, (ATTN_BLOCK, ATTN_OTHER), 1)
        for query_class in range(ATTN_CLASSES):
            for m in range(1, ATTN_CLASSES):
                delta = (col - ATTN_OTHER_LEAD - row) * ATTN_CLASSES + ((query_class + m) % ATTN_CLASSES - query_class)
                for ty, first_block in enumerate(type_starts):
                    bias_other_ref[query_class, ty, m - 1] = _attn_bias(
                        delta, first_block * ATTN_BLOCK + row, slope, n_rows)

    zeros = jnp.zeros((ATTN_PAD, HEAD_DIM), BF16)
    for c in range(ATTN_CLASSES):
        for pad_ref, src_ref in ((kpad_ref, k_ref), (vpad_ref, v_ref)):
            pad_ref[c, 0:ATTN_PAD, :] = zeros
            pad_ref[c, ATTN_PAD:ATTN_PAD + n_rows, :] = src_ref[c]
            pad_ref[c, ATTN_PAD + n_rows:ATTN_PAD + n_rows + ATTN_PAD, :] = zeros

    scale = HEAD_DIM ** -0.5

    def query_class(rq, carry):
        def block(n, carry):
            i0 = pl.multiple_of(n * ATTN_BLOCK, ATTN_BLOCK)
            ty = jnp.int32(0)
            for first_block in type_starts[1:]:
                ty = ty + jnp.asarray(n >= first_block, jnp.int32)
            qb = q_ref[rq, pl.ds(i0, ATTN_BLOCK), :]
            s = _dot_nt(qb, kpad_ref[rq, pl.ds(i0, ATTN_SAME), :]) * scale + bias_same_ref[ty]
            mx = _lane_reduce([s], jnp.maximum, jnp.max)
            p = jnp.exp(s - mx)
            den = _lane_reduce([p], jnp.add, jnp.sum)
            acc = jnp.dot(p.astype(BF16), vpad_ref[rq, pl.ds(i0, ATTN_SAME), :], preferred_element_type=F32)
            start = pl.multiple_of(i0 + (ATTN_PAD - ATTN_OTHER_LEAD), ATTN_OTHER_LEAD)
            scores, values = [], []
            for m in range(1, ATTN_CLASSES):
                rk = lax.rem(rq + m, ATTN_CLASSES)
                s = _dot_nt(qb, kpad_ref[rk, pl.ds(start, ATTN_OTHER), :]) * scale
                scores.append(s + bias_other_ref[rq, ty, m - 1])
                values.append(vpad_ref[rk, pl.ds(start, ATTN_OTHER), :])
            mx_new = jnp.maximum(mx, _lane_reduce(scores, jnp.maximum, jnp.max))
            keep = jnp.exp(mx - mx_new)
            probs = [jnp.exp(s - mx_new) for s in scores]
            den = den * keep + _lane_reduce(probs, jnp.add, jnp.sum)
            acc = acc * keep
            for p, v in zip(probs, values):
                acc = acc + jnp.dot(p.astype(BF16), v, preferred_element_type=F32)
            o_ref[rq, pl.ds(i0, ATTN_BLOCK), :] = (acc / den).astype(BF16)
            return carry

        return lax.fori_loop(0, n_rows // ATTN_BLOCK, block, carry, unroll=ATTN_UNROLL)

    lax.fori_loop(0, ATTN_CLASSES, query_class, 0)


def _attn(qkv, batch, seq):
    n_rows = seq // ATTN_CLASSES
    assert seq % (ATTN_HALF * max(ATTN_DILATIONS)) == 0 and seq >= 2 * ATTN_HALF * max(ATTN_DILATIONS)
    assert n_rows % (ATTN_BLOCK * ATTN_UNROLL) == 0
    type_starts = _attn_block_types(n_rows)
    n_types = len(type_starts)
    qkv = qkv.reshape(N_ATTN_PARTS, N_HEADS_ATTN, ATTN_CLASSES, batch, n_rows, HEAD_DIM)
    slopes = 2.0 ** (-8.0 * (jnp.arange(N_HEADS_ATTN, dtype=F32) + 1.0) / N_HEADS_ATTN)
    slopes = jnp.broadcast_to(slopes[:, None, None], (N_HEADS_ATTN, 1, HEAD_DIM))

    def part(p):
        return pl.BlockSpec((None, None, ATTN_CLASSES, None, n_rows, HEAD_DIM), lambda h, b: (p, h, 0, b, 0, 0))

    padded = n_rows + 2 * ATTN_PAD
    out = pl.pallas_call(
        functools.partial(_attn_kernel, n_rows=n_rows, type_starts=type_starts),
        grid=(N_HEADS_ATTN, batch),
        in_specs=[pl.BlockSpec((None, 1, HEAD_DIM), lambda h, b: (h, 0, 0)), part(0), part(1), part(2)],
        out_specs=pl.BlockSpec((None, ATTN_CLASSES, None, n_rows, HEAD_DIM), lambda h, b: (h, 0, b, 0, 0)),
        out_shape=jax.ShapeDtypeStruct((N_HEADS_ATTN, ATTN_CLASSES, batch, n_rows, HEAD_DIM), BF16),
        scratch_shapes=[
            pltpu.VMEM((ATTN_CLASSES, padded, HEAD_DIM), BF16),
            pltpu.VMEM((ATTN_CLASSES, padded, HEAD_DIM), BF16),
            pltpu.VMEM((n_types, ATTN_BLOCK, ATTN_SAME), F32),
            pltpu.VMEM((ATTN_CLASSES, n_types, ATTN_CLASSES - 1, ATTN_BLOCK, ATTN_OTHER), F32),
        ],
        compiler_params=pltpu.CompilerParams(
            dimension_semantics=("arbitrary", "arbitrary"), vmem_limit_bytes=VMEM_LIMIT_BYTES),
        name="attn",
    )(slopes, qkv, qkv, qkv)
    return out.reshape(ATTN_SLABS, batch * n_rows, HEAD_DIM)


def _out_proj_kernel(hg_ref, at_ref, w_ref, x_ref, g_ref, b_ref, o_ref, y_ref):
    _store_lane_tiles(y_ref, ALPHA * x_ref[...]
                      + jnp.dot(hg_ref[...], w_ref[0:HGRN_WIDTH, :], preferred_element_type=F32))
    rows = at_ref.shape[1]
    n_tiles = y_ref.shape[0]
    for c in range(ATTN_CLASSES):
        at_c = jnp.concatenate([at_ref[h * ATTN_CLASSES + c] for h in range(N_HEADS_ATTN)], axis=-1)
        mix = jnp.dot(at_c, w_ref[HGRN_WIDTH:HGRN_WIDTH + ATTN_WIDTH, :], preferred_element_type=F32)
        cls_rows = pl.ds(c, rows, stride=ATTN_CLASSES)
        y = jnp.concatenate([y_ref[j, cls_rows, :] for j in range(n_tiles)], axis=-1) + mix
        y = _layer_norm(y, g_ref[...], b_ref[...])
        for j in range(n_tiles):
            y_ref[j, cls_rows, :] = y[:, j * LANES:(j + 1) * LANES]
    o_ref[...] = jnp.concatenate([y_ref[j] for j in range(n_tiles)], axis=-1)


def _out_proj(hg, at4, w_out, x, g, b, *, tm=512):
    t, d = x.shape
    tok = lambda i: (i, 0)
    const = lambda i: (0, 0)
    return pl.pallas_call(
        _out_proj_kernel,
        grid=(t // tm,),
        in_specs=[
            pl.BlockSpec((tm, HGRN_WIDTH), tok),
            pl.BlockSpec((ATTN_SLABS, tm // ATTN_CLASSES, HEAD_DIM), lambda i: (0, i, 0)),
            pl.BlockSpec((HGRN_WIDTH + ATTN_WIDTH, d), const),
            pl.BlockSpec((tm, d), tok),
            pl.BlockSpec((1, d), const),
            pl.BlockSpec((1, d), const),
        ],
        out_specs=pl.BlockSpec((tm, d), tok),
        out_shape=jax.ShapeDtypeStruct((t, d), F32),
        scratch_shapes=[pltpu.VMEM((d // LANES, tm, LANES), F32)],
        compiler_params=pltpu.CompilerParams(
            dimension_semantics=("parallel",), vmem_limit_bytes=VMEM_LIMIT_BYTES),
        name="out_proj",
    )(hg, at4, w_out, x, g, b)


def _encoder_layer(x, w, layer):
    batch, seq, d = x.shape
    h = x.reshape(batch * seq, d)
    h, hb = _ffn(h, w["ffn1_w_gate"][layer], w["ffn1_w_up"][layer], w["ffn1_w_down"][layer],
                 w["ln1_g"][layer:layer + 1], w["ln1_b"][layer:layer + 1], emit_bf16=True)
    hg, qkv = _in_proj(hb, w["w_in"][layer])
    hgrn_out = _hgrn(hg, w["hgrn_lb_fwd"], w["hgrn_lb_bwd"], w["hgrn_norm_g"][layer], batch, seq)
    attn_out = _attn(qkv, batch, seq)
    h = _out_proj(hgrn_out.reshape(batch * seq, HGRN_WIDTH), attn_out, w["w_out"][layer], h,
                  w["ln2_g"][layer:layer + 1], w["ln2_b"][layer:layer + 1])
    h, = _ffn(h, w["ffn2_w_gate"][layer], w["ffn2_w_up"][layer], w["ffn2_w_down"][layer],
              w["ln3_g"][layer:layer + 1], w["ln3_b"][layer:layer + 1], emit_bf16=False)
    return h.reshape(batch, seq, d)


_MATMUL_WEIGHTS = ("ffn1_w_gate", "ffn1_w_up", "ffn1_w_down", "w_in", "w_out",
                   "ffn2_w_gate", "ffn2_w_up", "ffn2_w_down")


def kernel(x_prompt, x_sample, ln1_g, ln1_b, ffn1_w_gate, ffn1_w_up, ffn1_w_down, ln2_g, ln2_b, w_in,
           hgrn_lb_fwd, hgrn_lb_bwd, hgrn_norm_g, w_out, ln3_g, ln3_b, ffn2_w_gate, ffn2_w_up, ffn2_w_down):
    w = dict(ln1_g=ln1_g, ln1_b=ln1_b, ffn1_w_gate=ffn1_w_gate, ffn1_w_up=ffn1_w_up, ffn1_w_down=ffn1_w_down,
             ln2_g=ln2_g, ln2_b=ln2_b, w_in=w_in, hgrn_lb_fwd=hgrn_lb_fwd, hgrn_lb_bwd=hgrn_lb_bwd,
             hgrn_norm_g=hgrn_norm_g, w_out=w_out, ln3_g=ln3_g, ln3_b=ln3_b,
             ffn2_w_gate=ffn2_w_gate, ffn2_w_up=ffn2_w_up, ffn2_w_down=ffn2_w_down)
    for name in _MATMUL_WEIGHTS:
        w[name] = w[name].astype(BF16)
    outs = []
    for x in (x_prompt, x_sample):
        for layer in range(DEPTH):
            x = _encoder_layer(x, w, layer)
        outs.append(x)
    return tuple(outs)
```
